```python
import jax, jax.numpy as jnp
from jax import lax
import numpy as np

D_MODEL = 1024
BATCH = 8
SEQ = 8192
DEPTH = 2

D_MIX = D_MODEL
RET_HEADS = 4
RET_QK_DIM = 64
RET_V_DIM = 128
RET_CHUNK = 256
RET_WIDTH = RET_HEADS * RET_V_DIM
POOL_GROUPS = 4
POOL_WINDOWS = (2, 4, 8, 16)
POOL_GROUP_DIM = 64
POOL_WIDTH = POOL_GROUPS * POOL_GROUP_DIM
MOBA_HEADS = 4
MOBA_HEAD_DIM = 64
MOBA_WIDTH = MOBA_HEADS * MOBA_HEAD_DIM
MOBA_BLOCK = 256
MOBA_TOPK = 3
MOBA_Q_BLOCK = 32
IN_COLS = 2 * RET_HEADS * RET_QK_DIM + 2 * RET_WIDTH + POOL_WIDTH + 3 * MOBA_WIDTH
D_FF = 2816
PLE_DIM = 256
EPS = 1e-6

kernel_name = "hybrid_retention_pool_moba_macaron_block"


def rmsnorm(x, g):
    xf = x.astype(jnp.float32)
    y = xf * lax.rsqrt(jnp.mean(xf * xf, axis=-1, keepdims=True) + EPS)
    return (y * g.astype(jnp.float32)).astype(x.dtype)


def swiglu(h, w_gate, w_up, w_down):
    return (jax.nn.silu(h @ w_gate) * (h @ w_up)) @ w_down


def pad_to_multiple(t, mult, axis):
    n = t.shape[axis]
    pad = (-n) % mult
    if pad == 0:
        return t
    widths = [(0, 0)] * t.ndim
    widths[axis] = (0, pad)
    return jnp.pad(t, widths)


def rotate_every_two(t):
    t1 = t[..., ::2]
    t2 = t[..., 1::2]
    return jnp.stack((-t2, t1), axis=-1).reshape(t.shape)


def retention(q, k, v, g):
    B, S, H, dk = q.shape
    dv = v.shape[-1]
    C = RET_CHUNK
    q = pad_to_multiple(q.astype(jnp.float32), C, 1)
    k = pad_to_multiple(k.astype(jnp.float32), C, 1)
    v = pad_to_multiple(v.astype(jnp.float32), C, 1)
    Sp = q.shape[1]
    NC = Sp // C
    pos = jnp.arange(Sp, dtype=jnp.float32)
    angle = 1.0 / (10000.0 ** jnp.linspace(0.0, 1.0, dk // 2, dtype=jnp.float32))
    angle = jnp.repeat(angle, 2)
    ang = pos[:, None] * angle[None, :]
    sin = jnp.sin(ang)[:, None, :]
    cos = jnp.cos(ang)[:, None, :]
    q = q * cos + rotate_every_two(q) * sin
    k = (k * cos + rotate_every_two(k) * sin) * (dk ** -0.5)
    log_gamma = jnp.log(1.0 - jnp.power(2.0, -5.0 - jnp.arange(H, dtype=jnp.float32)))
    idx = jnp.arange(C, dtype=jnp.float32)
    diff = idx[:, None] - idx[None, :]
    dmask = jnp.where(diff >= 0, jnp.exp(log_gamma[:, None, None] * jnp.maximum(diff, 0.0)), 0.0)
    xi = jnp.exp(log_gamma[:, None] * (idx + 1.0))[None, :, :, None]
    zeta = jnp.exp(log_gamma[:, None] * (C - 1.0 - idx))[None, :, :, None]
    gC = jnp.exp(log_gamma * C)[None, :, None, None]

    def to_chunks(t):
        return t.reshape(B, NC, C, H, t.shape[-1]).transpose(1, 0, 3, 2, 4)

    def step(R, inp):
        qj, kj, vj = inp
        inner = jnp.einsum('bhid,bhmd->bhim', qj, kj) * dmask
        out = (jnp.einsum('bhim,bhme->bhie', inner, vj)
               + jnp.einsum('bhid,bhde->bhie', qj, R) * xi)
        R = gC * R + jnp.einsum('bhmd,bhme->bhde', kj * zeta, vj)
        return R, out

    R0 = jnp.zeros((B, H, dk, dv), jnp.float32)
    _, o = lax.scan(step, R0, (to_chunks(q), to_chunks(k), to_chunks(v)))
    o = o.transpose(1, 0, 3, 2, 4).reshape(B, Sp, H, dv)[:, :S]
    mu = jnp.mean(o, axis=-1, keepdims=True)
    var = jnp.mean(jnp.square(o - mu), axis=-1, keepdims=True)
    o = (o - mu) * lax.rsqrt(var + EPS)
    o = o * jax.nn.silu(g.astype(jnp.float32))
    return o.reshape(B, S, H * dv)


def multiscale_pool(u, w, scale):
    B, S, _ = u.shape
    uf = u.astype(jnp.float32).reshape(B, S, POOL_GROUPS, POOL_GROUP_DIM)
    cs = jnp.concatenate([jnp.zeros((B, 1, POOL_GROUPS, POOL_GROUP_DIM), jnp.float32),
                          jnp.cumsum(uf, axis=1)], axis=1)
    t = jnp.arange(S)
    outs = []
    for gi, win in enumerate(POOL_WINDOWS):
        start = jnp.maximum(t + 1 - win, 0)
        cnt = (t + 1 - start).astype(jnp.float32)
        csg = cs[:, :, gi]
        wsum = csg[:, 1:] - csg[:, start]
        outs.append(wsum / cnt[None, :, None] - uf[:, :, gi])
    pooled = jnp.stack(outs, axis=2)
    y = jnp.einsum('bsgc,gcd->bsgd', pooled, w.astype(jnp.float32)).reshape(B, S, POOL_WIDTH)
    return y * scale.astype(jnp.float32)


def moba_attention(q, k, v):
    B, S, H, dh = q.shape
    BLK = MOBA_BLOCK
    QB = MOBA_Q_BLOCK
    q = pad_to_multiple(q.astype(jnp.float32).transpose(0, 2, 1, 3), BLK, 2) * (dh ** -0.5)
    k = pad_to_multiple(k.astype(jnp.float32).transpose(0, 2, 1, 3), BLK, 2)
    v = pad_to_multiple(v.astype(jnp.float32).transpose(0, 2, 1, 3), BLK, 2)
    Sp = q.shape[2]
    NB = Sp // BLK
    kb = k.reshape(B, H, NB, BLK, dh)
    vb = v.reshape(B, H, NB, BLK, dh)
    kmean = jnp.mean(kb, axis=3)
    gate = jnp.einsum('bhsd,bhnd->bhsn', q, kmean)
    qblk = jnp.arange(Sp) // BLK
    past = jnp.arange(NB)[None, :] < qblk[:, None]
    gate = jnp.where(past, gate, -jnp.inf)
    k_sel = min(MOBA_TOPK, NB)
    _, sel = lax.top_k(gate, k_sel)
    nqb = Sp // QB
    qs = q.reshape(B, H, nqb, QB, dh).transpose(2, 0, 1, 3, 4)
    sels = sel.reshape(B, H, nqb, QB, k_sel).transpose(2, 0, 1, 3, 4)
    bi = jnp.arange(B)[:, None, None, None]
    hi = jnp.arange(H)[None, :, None, None]

    def one_block(args):
        j, qj, selj = args
        q0 = j * QB
        own = q0 // BLK
        qpos = q0 + jnp.arange(QB)
        kpos = own * BLK + jnp.arange(BLK)
        k_own = lax.dynamic_index_in_dim(kb, own, axis=2, keepdims=False)
        v_own = lax.dynamic_index_in_dim(vb, own, axis=2, keepdims=False)
        k_g = kb[bi, hi, selj]
        v_g = vb[bi, hi, selj]
        s_own = jnp.einsum('bhqd,bhkd->bhqk', qj, k_own)
        s_own = jnp.where(kpos[None, :] <= qpos[:, None], s_own, -jnp.inf)
        s_sel = jnp.einsum('bhqd,bhqnkd->bhqnk', qj, k_g)
        valid = jnp.arange(k_sel) < own
        s_sel = jnp.where(valid[:, None], s_sel, -jnp.inf).reshape(B, H, QB, k_sel * BLK)
        probs = jax.nn.softmax(jnp.concatenate([s_sel, s_own], axis=-1), axis=-1)
        p_sel = probs[..., :k_sel * BLK].reshape(B, H, QB, k_sel, BLK)
        p_own = probs[..., k_sel * BLK:]
        return (jnp.einsum('bhqnk,bhqnkd->bhqd', p_sel, v_g)
                + jnp.einsum('bhqk,bhkd->bhqd', p_own, v_own))

    o = lax.map(one_block, (jnp.arange(nqb), qs, sels))
    o = o.transpose(1, 0, 3, 2, 4).reshape(B, Sp, H * dh)[:, :S]
    return o


def setup_inputs(seed: int = 0) -> dict:
    key = jax.random.key(seed)
    ks = jax.random.split(key, 20)
    f32 = jnp.float32

    def nrm(k, shape, scale):
        return jax.random.normal(k, shape, f32) * scale

    return {
        "x": nrm(ks[0], (BATCH, SEQ, D_MODEL), 1.0),
        "p": nrm(ks[1], (DEPTH, BATCH, SEQ, PLE_DIM), 1.0),
        "norm_ffn1": 1.0 + nrm(ks[2], (DEPTH, D_MODEL), 0.02),
        "ffn1_w_gate": nrm(ks[3], (DEPTH, D_MODEL, D_FF), D_MODEL ** -0.5),
        "ffn1_w_up": nrm(ks[4], (DEPTH, D_MODEL, D_FF), D_MODEL ** -0.5),
        "ffn1_w_down": nrm(ks[5], (DEPTH, D_FF, D_MODEL), D_FF ** -0.5),
        "norm_mix": 1.0 + nrm(ks[6], (DEPTH, D_MODEL), 0.02),
        "w_in": nrm(ks[7], (DEPTH, D_MODEL, IN_COLS), D_MODEL ** -0.5),
        "pool_w": nrm(ks[8], (DEPTH, POOL_GROUPS, POOL_GROUP_DIM, POOL_GROUP_DIM), POOL_GROUP_DIM ** -0.5),
        "pool_scale": 1.0 + nrm(ks[9], (DEPTH, POOL_WIDTH), 0.1),
        "w_out": nrm(ks[10], (DEPTH, D_MIX, D_MODEL), D_MIX ** -0.5),
        "norm_ffn2": 1.0 + nrm(ks[11], (DEPTH, D_MODEL), 0.02),
        "ffn2_w_gate": nrm(ks[12], (DEPTH, D_MODEL, D_FF), D_MODEL ** -0.5),
        "ffn2_w_up": nrm(ks[13], (DEPTH, D_MODEL, D_FF), D_MODEL ** -0.5),
        "ffn2_w_down": nrm(ks[14], (DEPTH, D_FF, D_MODEL), D_FF ** -0.5),
        "norm_ple": 1.0 + nrm(ks[15], (DEPTH, D_MODEL), 0.02),
        "ple_w_gate": nrm(ks[16], (DEPTH, D_MODEL, D_MODEL), D_MODEL ** -0.5),
        "ple_w_proj": nrm(ks[17], (DEPTH, PLE_DIM, D_MODEL), PLE_DIM ** -0.5),
        "norm_final": 1.0 + nrm(ks[18], (D_MODEL,), 0.02),
    }


def reference(x, p, norm_ffn1, ffn1_w_gate, ffn1_w_up, ffn1_w_down, norm_mix, w_in,
              pool_w, pool_scale, w_out, norm_ffn2, ffn2_w_gate, ffn2_w_up, ffn2_w_down,
              norm_ple, ple_w_gate, ple_w_proj, norm_final):
    B, S, _ = x.shape
    dqk = RET_HEADS * RET_QK_DIM
    offs = np.cumsum([dqk, dqk, RET_WIDTH, RET_WIDTH, POOL_WIDTH, MOBA_WIDTH, MOBA_WIDTH]).tolist()
    for i in range(DEPTH):
        h = rmsnorm(x, norm_ffn1[i])
        x = x + 0.5 * swiglu(h, ffn1_w_gate[i], ffn1_w_up[i], ffn1_w_down[i])
        h = rmsnorm(x, norm_mix[i])
        z = h @ w_in[i]
        rq, rk, rv, rg, pu, mq, mk, mv = jnp.split(z, offs, axis=-1)
        y_ret = retention(rq.reshape(B, S, RET_HEADS, RET_QK_DIM),
                          rk.reshape(B, S, RET_HEADS, RET_QK_DIM),
                          rv.reshape(B, S, RET_HEADS, RET_V_DIM),
                          rg.reshape(B, S, RET_HEADS, RET_V_DIM))
        y_pool = multiscale_pool(pu, pool_w[i], pool_scale[i])
        y_moba = moba_attention(mq.reshape(B, S, MOBA_HEADS, MOBA_HEAD_DIM),
                                mk.reshape(B, S, MOBA_HEADS, MOBA_HEAD_DIM),
                                mv.reshape(B, S, MOBA_HEADS, MOBA_HEAD_DIM))
        y = jnp.concatenate([y_ret, y_pool, y_moba], axis=-1).astype(x.dtype)
        x = x + y @ w_out[i]
        h = rmsnorm(x, norm_ffn2[i])
        x = x + 0.5 * swiglu(h, ffn2_w_gate[i], ffn2_w_up[i], ffn2_w_down[i])
        h = rmsnorm(x, norm_ple[i])
        x = x + jax.nn.sigmoid(h @ ple_w_gate[i]) * (p[i] @ ple_w_proj[i])
    return rmsnorm(x, norm_final)
```

```python
import functools

import jax
import jax.numpy as jnp
import numpy as np
from jax import lax
from jax.experimental import pallas as pl
from jax.experimental.pallas import tpu as pltpu

D_MODEL = 1024
D_FF = 2816
RET_HEADS = 4
RET_QK_DIM = 64
RET_V_DIM = 128
RET_CHUNK = 256
RET_QK = RET_HEADS * RET_QK_DIM
RET_WIDTH = RET_HEADS * RET_V_DIM
POOL_GROUPS = 4
POOL_WINDOWS = (2, 4, 8, 16)
POOL_GROUP_DIM = 64
POOL_WIDTH = POOL_GROUPS * POOL_GROUP_DIM
POOL_HALO = 16
MOBA_HEADS = 4
MOBA_HEAD_DIM = 64
MOBA_WIDTH = MOBA_HEADS * MOBA_HEAD_DIM
MOBA_BLOCK = 256
MOBA_TOPK = 3
PLE_DIM = 256
EPS = 1e-6

FF_CHUNK = 256
TOKEN_TILE = 512
VMEM_LIMIT = 56 * 1024 * 1024

BF16 = jnp.bfloat16
F32 = jnp.float32
NEG_INF = float("-inf")


def _params(*semantics):
    return pltpu.CompilerParams(dimension_semantics=semantics, vmem_limit_bytes=VMEM_LIMIT)


def _resident(shape, index_map):
    return pl.BlockSpec(shape, index_map, pipeline_mode=pl.Buffered(1))


def _rms(x, gain):
    return x * lax.rsqrt(jnp.mean(x * x, axis=-1, keepdims=True) + EPS) * gain


def _dot(a, b):
    return jnp.dot(a, b, preferred_element_type=F32)


def _dot_nt(a, b):
    return lax.dot_general(a, b, (((1,), (1,)), ((), ())), preferred_element_type=F32)


def _dot_tn(a, b):
    return lax.dot_general(a, b, (((0,), (0,)), ((), ())), preferred_element_type=F32)


def _ffn_kernel(x_ref, gain_ref, wg_ref, wu_ref, wd_ref, o_ref, act_ref):
    x = x_ref[...]
    h = _rms(x, gain_ref[...]).astype(BF16)
    for c in range(D_FF // FF_CHUNK):
        cols = slice(c * FF_CHUNK, (c + 1) * FF_CHUNK)
        gate = _dot(h, wg_ref[:, cols])
        up = _dot(h, wu_ref[:, cols])
        act_ref[:, cols] = (gate * jax.nn.sigmoid(gate) * up).astype(BF16)
    o_ref[...] = x + 0.5 * _dot(act_ref[...], wd_ref[...])


def _ffn(x, gain, wg, wu, wd, layer):
    tokens = x.shape[0]
    tm = TOKEN_TILE
    return pl.pallas_call(
        _ffn_kernel,
        grid=(tokens // tm,),
        in_specs=[
            pl.BlockSpec((tm, D_MODEL), lambda i: (i, 0)),
            _resident((None, 1, D_MODEL), lambda i: (layer, 0, 0)),
            _resident((None, D_MODEL, D_FF), lambda i: (layer, 0, 0)),
            _resident((None, D_MODEL, D_FF), lambda i: (layer, 0, 0)),
            _resident((None, D_FF, D_MODEL), lambda i: (layer, 0, 0)),
        ],
        out_specs=pl.BlockSpec((tm, D_MODEL), lambda i: (i, 0)),
        out_shape=jax.ShapeDtypeStruct(x.shape, F32),
        scratch_shapes=[pltpu.VMEM((tm, D_FF), BF16)],
        compiler_params=_params("parallel"),
        name="ffn",
    )(x, gain, wg, wu, wd)


def _rotate_every_two(t, even_lane):
    n = t.shape[-1]
    nxt = pltpu.roll(t, n - 1, 1)
    prv = pltpu.roll(t, 1, 1)
    return jnp.where(even_lane, -nxt, prv)


def _proj_kernel(x_ref, gain_ref, w_ref, cos_ref, sin_ref, poolw_ref, poolscale_ref,
                 rq_ref, rk_ref, rv_ref, sg_ref, ypool_ref, mqt_ref, mk_ref, mvt_ref,
                 halo_ref):
    tm = x_ref.shape[1]
    st = pl.program_id(1)
    h = _rms(x_ref[0], gain_ref[...]).astype(BF16)

    def cols(lo, width):
        return _dot(h, w_ref[:, lo:lo + width])

    cos = cos_ref[...]
    sin = sin_ref[...]
    even_lane = (lax.broadcasted_iota(jnp.int32, (tm, RET_QK), 1) % 2) == 0
    q = cols(0, RET_QK)
    rq_ref[0] = (q * cos + _rotate_every_two(q, even_lane) * sin).astype(BF16)
    k = cols(RET_QK, RET_QK)
    k = (k * cos + _rotate_every_two(k, even_lane) * sin) * (RET_QK_DIM ** -0.5)
    rk_ref[0] = k.astype(BF16)
    rv_ref[0] = cols(2 * RET_QK, RET_WIDTH).astype(BF16)
    g = cols(2 * RET_QK + RET_WIDTH, RET_WIDTH)
    sg_ref[0] = (g * jax.nn.sigmoid(g)).astype(BF16)

    off = 2 * RET_QK + 2 * RET_WIDTH
    u = cols(off, POOL_WIDTH)

    @pl.when(st == 0)
    def _():
        halo_ref[...] = jnp.zeros_like(halo_ref)

    ext = jnp.concatenate([halo_ref[...], u], axis=0)
    halo_ref[...] = u[tm - POOL_HALO:, :]
    lane = lax.broadcasted_iota(jnp.int32, (tm, POOL_WIDTH), 1)
    group = lane // POOL_GROUP_DIM
    wsum = None
    win_of_lane = None
    acc = ext
    for gi, win in enumerate(POOL_WINDOWS):
        acc = acc + pltpu.roll(acc, win // 2, 0)
        cur = acc[POOL_HALO:, :]
        wsum = cur if wsum is None else jnp.where(group == gi, cur, wsum)
        wl = jnp.full((tm, POOL_WIDTH), float(win), F32)
        win_of_lane = wl if win_of_lane is None else jnp.where(group == gi, wl, win_of_lane)
    pos = (st * tm + lax.broadcasted_iota(jnp.int32, (tm, POOL_WIDTH), 0) + 1).astype(F32)
    cnt = jnp.minimum(pos, win_of_lane)
    pooled = (wsum / cnt - u).astype(BF16)
    ypool_ref[0] = (_dot(pooled, poolw_ref[...]) * poolscale_ref[...]).astype(BF16)

    off += POOL_WIDTH
    mq = cols(off, MOBA_WIDTH) * (MOBA_HEAD_DIM ** -0.5)
    mqt_ref[0] = mq.T.astype(BF16)
    mk_ref[0] = cols(off + MOBA_WIDTH, MOBA_WIDTH).astype(BF16)
    mv = cols(off + 2 * MOBA_WIDTH, MOBA_WIDTH)
    for j in range(tm // MOBA_BLOCK):
        mvt_ref[0, j] = mv[j * MOBA_BLOCK:(j + 1) * MOBA_BLOCK, :].T.astype(BF16)


def _proj(x, gain, w_in, cos, sin, pool_w, pool_scale, layer):
    b, s, _ = x.shape
    tm = TOKEN_TILE
    in_cols = w_in.shape[-1]
    nb = s // MOBA_BLOCK
    bpt = tm // MOBA_BLOCK
    tok = lambda width: pl.BlockSpec((1, tm, width), lambda bi, si: (bi, si, 0))
    out_shape = [
        jax.ShapeDtypeStruct((b, s, RET_QK), BF16),
        jax.ShapeDtypeStruct((b, s, RET_QK), BF16),
        jax.ShapeDtypeStruct((b, s, RET_WIDTH), BF16),
        jax.ShapeDtypeStruct((b, s, RET_WIDTH), BF16),
        jax.ShapeDtypeStruct((b, s, POOL_WIDTH), BF16),
        jax.ShapeDtypeStruct((b, MOBA_WIDTH, s), BF16),
        jax.ShapeDtypeStruct((b, s, MOBA_WIDTH), BF16),
        jax.ShapeDtypeStruct((b, nb, MOBA_WIDTH, MOBA_BLOCK), BF16),
    ]
    out_specs = [
        tok(RET_QK), tok(RET_QK), tok(RET_WIDTH), tok(RET_WIDTH), tok(POOL_WIDTH),
        pl.BlockSpec((1, MOBA_WIDTH, tm), lambda bi, si: (bi, 0, si)),
        tok(MOBA_WIDTH),
        pl.BlockSpec((1, bpt, MOBA_WIDTH, MOBA_BLOCK), lambda bi, si: (bi, si, 0, 0)),
    ]
    return pl.pallas_call(
        _proj_kernel,
        grid=(b, s // tm),
        in_specs=[
            pl.BlockSpec((1, tm, D_MODEL), lambda bi, si: (bi, si, 0)),
            _resident((None, 1, D_MODEL), lambda bi, si: (layer, 0, 0)),
            _resident((None, D_MODEL, in_cols), lambda bi, si: (layer, 0, 0)),
            pl.BlockSpec((tm, RET_QK), lambda bi, si: (si, 0)),
            pl.BlockSpec((tm, RET_QK), lambda bi, si: (si, 0)),
            _resident((None, POOL_WIDTH, POOL_WIDTH), lambda bi, si: (layer, 0, 0)),
            _resident((None, 1, POOL_WIDTH), lambda bi, si: (layer, 0, 0)),
        ],
        out_specs=out_specs,
        out_shape=out_shape,
        scratch_shapes=[pltpu.VMEM((POOL_HALO, POOL_WIDTH), F32)],
        compiler_params=_params("parallel", "arbitrary"),
        name="proj",
    )(x, gain, w_in, cos, sin, pool_w, pool_scale)


def _retention_kernel(q_ref, k_ref, v_ref, sg_ref, dmask_ref, xi_ref, zeta_ref, decay_ref,
                      ondiag_ref, o_ref, state_ref):
    @pl.when(pl.program_id(1) == 0)
    def _():
        state_ref[...] = jnp.zeros_like(state_ref)

    q = q_ref[0]
    k = k_ref[0]
    v = v_ref[0]
    chunk = q.shape[0]
    state = state_ref[...]
    cross = _dot(q, state.astype(BF16)) * xi_ref[...]
    lane_head = lax.broadcasted_iota(jnp.int32, (chunk, RET_QK), 1) // RET_QK_DIM
    for hd in range(RET_HEADS):
        vcols = slice(hd * RET_V_DIM, (hd + 1) * RET_V_DIM)
        qh = jnp.where(lane_head == hd, q, jnp.zeros_like(q))
        inner = (_dot_nt(qh, k) * dmask_ref[hd]).astype(BF16)
        o = _dot(inner, v[:, vcols]) + cross[:, vcols]
        mu = jnp.mean(o, axis=-1, keepdims=True)
        var = jnp.mean(jnp.square(o - mu), axis=-1, keepdims=True)
        o = (o - mu) * lax.rsqrt(var + EPS)
        o_ref[0, :, vcols] = (o * sg_ref[0, :, vcols].astype(F32)).astype(BF16)

    kz = (k.astype(F32) * zeta_ref[...]).astype(BF16)
    state_ref[...] = decay_ref[...] * state + ondiag_ref[...] * _dot_tn(kz, v)


def _retention(rq, rk, rv, sg, dmask, xi, zeta, decay, on_diag):
    b, s, _ = rq.shape
    c = RET_CHUNK
    tok = lambda width: pl.BlockSpec((1, c, width), lambda bi, ci: (bi, ci, 0))
    const = lambda a: _resident(a.shape, lambda bi, ci: (0,) * a.ndim)
    return pl.pallas_call(
        _retention_kernel,
        grid=(b, s // c),
        in_specs=[tok(RET_QK), tok(RET_QK), tok(RET_WIDTH), tok(RET_WIDTH),
                  const(dmask), const(xi), const(zeta), const(decay), const(on_diag)],
        out_specs=tok(RET_WIDTH),
        out_shape=jax.ShapeDtypeStruct((b, s, RET_WIDTH), BF16),
        scratch_shapes=[pltpu.VMEM((RET_QK, RET_WIDTH), F32)],
        compiler_params=_params("parallel", "arbitrary"),
        name="retention",
    )(rq, rk, rv, sg, dmask, xi, zeta, decay, on_diag)


def _retention_tables():
    c = RET_CHUNK
    log_gamma = jnp.log(1.0 - jnp.power(2.0, -5.0 - jnp.arange(RET_HEADS, dtype=F32)))
    idx = jnp.arange(c, dtype=F32)
    diff = idx[:, None] - idx[None, :]
    dmask = jnp.where(diff >= 0, jnp.exp(log_gamma[:, None, None] * jnp.maximum(diff, 0.0)), 0.0)
    xi = jnp.exp(log_gamma[:, None] * (idx + 1.0))
    zeta = jnp.exp(log_gamma[:, None] * (c - 1.0 - idx))
    g_chunk = jnp.exp(log_gamma * c)
    xi_t = jnp.repeat(xi.T, RET_V_DIM, axis=1)
    zeta_t = jnp.repeat(zeta.T, RET_QK_DIM, axis=1)
    head_of_row = np.arange(RET_QK) // RET_QK_DIM
    head_of_col = np.arange(RET_WIDTH) // RET_V_DIM
    on_diag = jnp.asarray(head_of_row[:, None] == head_of_col[None, :], dtype=F32)
    decay = on_diag * jnp.repeat(g_chunk, RET_QK_DIM)[:, None]
    return dmask, xi_t, zeta_t, decay, on_diag


def _rotation_tables(seq):
    pos = jnp.arange(seq, dtype=F32)
    angle = 1.0 / (10000.0 ** jnp.linspace(0.0, 1.0, RET_QK_DIM // 2, dtype=F32))
    angle = jnp.repeat(angle, 2)
    ang = pos[:, None] * angle[None, :]
    return jnp.tile(jnp.cos(ang), (1, RET_HEADS)), jnp.tile(jnp.sin(ang), (1, RET_HEADS))


def _moba_kernel(qt_ref, k_ref, vt_ref, o_ref, kmean_ref, sel_ref, out_ref):
    qi = pl.program_id(1)
    nb = kmean_ref.shape[0]
    blk = MOBA_BLOCK
    tq = qt_ref.shape[2]

    @pl.when(qi == 0)
    def _():
        for j in range(nb):
            kb = k_ref[0, j * blk:(j + 1) * blk, :].astype(F32)
            kmean_ref[j:j + 1, :] = jnp.mean(kb, axis=0, keepdims=True)

    qt = qt_ref[0]
    row_head = lax.broadcasted_iota(jnp.int32, (MOBA_WIDTH, tq), 0) // MOBA_HEAD_DIM
    kmean = kmean_ref[...].astype(BF16)
    blk_id = lax.broadcasted_iota(jnp.int32, (nb, tq), 0)
    key_pos = lax.broadcasted_iota(jnp.int32, (blk, tq), 0)
    qry_pos = lax.broadcasted_iota(jnp.int32, (blk, tq), 1)
    k_own = k_ref[0, pl.ds(pl.multiple_of(qi * blk, blk), blk), :]

    for hd in range(MOBA_HEADS):
        rows = slice(hd * MOBA_HEAD_DIM, (hd + 1) * MOBA_HEAD_DIM)
        qth = jnp.where(row_head == hd, qt, jnp.zeros_like(qt))

        past = blk_id < qi
        gate = jnp.where(past, _dot(kmean, qth), NEG_INF)
        sel = jnp.zeros((nb, tq), F32)
        for _ in range(MOBA_TOPK):
            best = jnp.max(gate, axis=0, keepdims=True)
            idx = jnp.min(jnp.where(gate == best, blk_id, nb), axis=0, keepdims=True)
            hit = blk_id == idx
            sel = jnp.where(jnp.logical_and(hit, past), 1.0, sel)
            gate = jnp.where(hit, NEG_INF, gate)
        sel_ref[...] = sel

        s = jnp.where(key_pos <= qry_pos, _dot(k_own, qth), NEG_INF)
        m = jnp.max(s, axis=0, keepdims=True)
        p = jnp.exp(s - m)
        l = jnp.sum(p, axis=0, keepdims=True)
        acc = _dot(vt_ref[0, qi, rows, :], p.astype(BF16))

        def body(j, carry):
            m, l, acc = carry
            kj = k_ref[0, pl.ds(pl.multiple_of(j * blk, blk), blk), :]
            chosen = sel_ref[pl.ds(j, 1), :] > 0.0
            s = jnp.where(chosen, _dot(kj, qth), NEG_INF)
            m_new = jnp.maximum(m, jnp.max(s, axis=0, keepdims=True))
            alpha = jnp.exp(m - m_new)
            p = jnp.exp(s - m_new)
            l = alpha * l + jnp.sum(p, axis=0, keepdims=True)
            acc = alpha * acc + _dot(vt_ref[0, j, rows, :], p.astype(BF16))
            return m_new, l, acc

        m, l, acc = lax.fori_loop(0, qi, body, (m, l, acc))
        out_ref[rows, :] = acc / l

    o_ref[0] = out_ref[...].T.astype(BF16)


def _moba(mqt, mk, mvt):
    b, s, _ = mk.shape
    nb = s // MOBA_BLOCK
    tq = MOBA_BLOCK
    return pl.pallas_call(
        _moba_kernel,
        grid=(b, nb),
        in_specs=[
            pl.BlockSpec((1, MOBA_WIDTH, tq), lambda bi, qi: (bi, 0, qi)),
            pl.BlockSpec((1, s, MOBA_WIDTH), lambda bi, qi: (bi, 0, 0)),
            pl.BlockSpec((1, nb, MOBA_WIDTH, MOBA_BLOCK), lambda bi, qi: (bi, 0, 0, 0)),
        ],
        out_specs=pl.BlockSpec((1, tq, MOBA_WIDTH), lambda bi, qi: (bi, qi, 0)),
        out_shape=jax.ShapeDtypeStruct((b, s, MOBA_WIDTH), BF16),
        scratch_shapes=[
            pltpu.VMEM((nb, MOBA_WIDTH), F32),
            pltpu.VMEM((nb, tq), F32),
            pltpu.VMEM((MOBA_WIDTH, tq), F32),
        ],
        compiler_params=_params("parallel", "arbitrary"),
        name="moba",
    )(mqt, mk, mvt)


def _outproj_kernel(x_ref, yr_ref, yp_ref, ym_ref, w_ref, o_ref):
    acc = _dot(yr_ref[...], w_ref[:RET_WIDTH, :])
    acc += _dot(yp_ref[...], w_ref[RET_WIDTH:RET_WIDTH + POOL_WIDTH, :])
    acc += _dot(ym_ref[...], w_ref[RET_WIDTH + POOL_WIDTH:, :])
    o_ref[...] = x_ref[...] + acc


def _outproj(x, y_ret, y_pool, y_moba, w_out, layer):
    tokens = x.shape[0]
    tm = TOKEN_TILE
    tok = lambda width: pl.BlockSpec((tm, width), lambda i: (i, 0))
    return pl.pallas_call(
        _outproj_kernel,
        grid=(tokens // tm,),
        in_specs=[tok(D_MODEL), tok(RET_WIDTH), tok(POOL_WIDTH), tok(MOBA_WIDTH),
                  _resident((None, D_MODEL, D_MODEL), lambda i: (layer, 0, 0))],
        out_specs=tok(D_MODEL),
        out_shape=jax.ShapeDtypeStruct(x.shape, F32),
        compiler_params=_params("parallel"),
        name="outproj",
    )(x, y_ret, y_pool, y_moba, w_out)


def _ple_kernel(x_ref, p_ref, gain_ref, wgate_ref, wproj_ref, final_gain_ref, o_ref, *, final):
    x = x_ref[...]
    h = _rms(x, gain_ref[...]).astype(BF16)
    gate = jax.nn.sigmoid(_dot(h, wgate_ref[...]))
    x = x + gate * _dot(p_ref[...].astype(BF16), wproj_ref[...])
    if final:
        x = _rms(x, final_gain_ref[...])
    o_ref[...] = x


def _ple(x, p, gain, w_gate, w_proj, final_gain, layer, final):
    tokens = x.shape[0]
    tm = TOKEN_TILE
    return pl.pallas_call(
        functools.partial(_ple_kernel, final=final),
        grid=(tokens // tm,),
        in_specs=[
            pl.BlockSpec((tm, D_MODEL), lambda i: (i, 0)),
            pl.BlockSpec((None, tm, PLE_DIM), lambda i: (layer, i, 0)),
            _resident((None, 1, D_MODEL), lambda i: (layer, 0, 0)),
            _resident((None, D_MODEL, D_MODEL), lambda i: (layer, 0, 0)),
            _resident((None, PLE_DIM, D_MODEL), lambda i: (layer, 0, 0)),
            _resident((1, D_MODEL), lambda i: (0, 0)),
        ],
        out_specs=pl.BlockSpec((tm, D_MODEL), lambda i: (i, 0)),
        out_shape=jax.ShapeDtypeStruct(x.shape, F32),
        compiler_params=_params("parallel"),
        name="ple",
    )(x, p, gain, w_gate, w_proj, final_gain)


def _block_diag(w):
    depth, groups, d, _ = w.shape
    eye = jnp.eye(groups, dtype=w.dtype)
    return jnp.einsum("lgcd,gh->lgchd", w, eye).reshape(depth, groups * d, groups * d)


def kernel(x, p, norm_ffn1, ffn1_w_gate, ffn1_w_up, ffn1_w_down, norm_mix, w_in, pool_w, pool_scale, w_out, norm_ffn2, ffn2_w_gate, ffn2_w_up, ffn2_w_down, norm_ple, ple_w_gate, ple_w_proj, norm_final):
    b, s, d = x.shape
    depth = w_in.shape[0]
    tokens = b * s
    bf = lambda w: w.astype(BF16)
    row = lambda g: g.reshape(g.shape[0], 1, g.shape[1])

    ffn1 = (row(norm_ffn1), bf(ffn1_w_gate), bf(ffn1_w_up), bf(ffn1_w_down))
    ffn2 = (row(norm_ffn2), bf(ffn2_w_gate), bf(ffn2_w_up), bf(ffn2_w_down))
    w_in_b, w_out_b = bf(w_in), bf(w_out)
    pool_w_b = bf(_block_diag(pool_w))
    ple_gate_b, ple_proj_b = bf(ple_w_gate), bf(ple_w_proj)
    norm_mix_r, norm_ple_r, pool_scale_r = row(norm_mix), row(norm_ple), row(pool_scale)
    final_gain = norm_final.reshape(1, d)
    p_flat = p.reshape(depth, tokens, PLE_DIM)
    cos, sin = _rotation_tables(s)
    ret_tables = _retention_tables()

    xf = x.reshape(tokens, d)
    for i in range(depth):
        xf = _ffn(xf, *ffn1, i)
        rq, rk, rv, sg, y_pool, mqt, mk, mvt = _proj(
            xf.reshape(b, s, d), norm_mix_r, w_in_b, cos, sin, pool_w_b, pool_scale_r, i)
        y_ret = _retention(rq, rk, rv, sg, *ret_tables)
        y_moba = _moba(mqt, mk, mvt)
        xf = _outproj(xf, y_ret.reshape(tokens, RET_WIDTH), y_pool.reshape(tokens, POOL_WIDTH),
                      y_moba.reshape(tokens, MOBA_WIDTH), w_out_b, i)
        xf = _ffn(xf, *ffn2, i)
        xf = _ple(xf, p_flat, norm_ple_r, ple_gate_b, ple_proj_b, final_gain, i, i == depth - 1)
    return xf.reshape(b, s, d)
```

```python
import functools

import jax
import jax.numpy as jnp
import numpy as np
from jax import lax
from jax.experimental import pallas as pl
from jax.experimental.pallas import tpu as pltpu

D_MODEL = 1024
D_FF = 2816
RET_HEADS = 4
RET_QK_DIM = 64
RET_V_DIM = 128
RET_CHUNK = 256
RET_QK = RET_HEADS * RET_QK_DIM
RET_WIDTH = RET_HEADS * RET_V_DIM
POOL_GROUPS = 4
POOL_WINDOWS = (2, 4, 8, 16)
POOL_GROUP_DIM = 64
POOL_WIDTH = POOL_GROUPS * POOL_GROUP_DIM
POOL_HALO = 16
MOBA_HEADS = 4
MOBA_HEAD_DIM = 64
MOBA_WIDTH = MOBA_HEADS * MOBA_HEAD_DIM
MOBA_BLOCK = 256
MOBA_TOPK = 3
MOBA_VT_ROWS = MOBA_HEAD_DIM + 16
MOBA_ROW_CHUNK = 64
MOBA_M_INIT = -(2.0 ** 100)
PLE_DIM = 256
EPS = 1e-6

FF_CHUNK = 256
TOKEN_TILE = 512
VMEM_LIMIT = 56 * 1024 * 1024

BF16 = jnp.bfloat16
F32 = jnp.float32
NEG_INF = float("-inf")
LOG2_E = 1.4426950408889634


def _params(*semantics):
    return pltpu.CompilerParams(dimension_semantics=semantics, vmem_limit_bytes=VMEM_LIMIT)


def _resident(shape, index_map):
    return pl.BlockSpec(shape, index_map, pipeline_mode=pl.Buffered(1))


def _rms(x, gain):
    return x * lax.rsqrt(jnp.mean(x * x, axis=-1, keepdims=True) + EPS) * gain


def _dot(a, b):
    return jnp.dot(a, b, preferred_element_type=F32)


def _dot_nt(a, b):
    return lax.dot_general(a, b, (((1,), (1,)), ((), ())), preferred_element_type=F32)


def _dot_tn(a, b):
    return lax.dot_general(a, b, (((0,), (0,)), ((), ())), preferred_element_type=F32)


def _ffn_kernel(x_ref, gain_ref, wg_ref, wu_ref, wd_ref, o_ref, act_ref):
    x = x_ref[...]
    h = _rms(x, gain_ref[...]).astype(BF16)
    for c in range(D_FF // FF_CHUNK):
        cols = slice(c * FF_CHUNK, (c + 1) * FF_CHUNK)
        gate = _dot(h, wg_ref[:, cols])
        up = _dot(h, wu_ref[:, cols])
        act_ref[:, cols] = (gate * jax.nn.sigmoid(gate) * up).astype(BF16)
    o_ref[...] = x + 0.5 * _dot(act_ref[...], wd_ref[...])


def _ffn(x, gain, wg, wu, wd, layer):
    tokens = x.shape[0]
    tm = TOKEN_TILE
    return pl.pallas_call(
        _ffn_kernel,
        grid=(tokens // tm,),
        in_specs=[
            pl.BlockSpec((tm, D_MODEL), lambda i: (i, 0)),
            _resident((None, 1, D_MODEL), lambda i: (layer, 0, 0)),
            _resident((None, D_MODEL, D_FF), lambda i: (layer, 0, 0)),
            _resident((None, D_MODEL, D_FF), lambda i: (layer, 0, 0)),
            _resident((None, D_FF, D_MODEL), lambda i: (layer, 0, 0)),
        ],
        out_specs=pl.BlockSpec((tm, D_MODEL), lambda i: (i, 0)),
        out_shape=jax.ShapeDtypeStruct(x.shape, F32),
        scratch_shapes=[pltpu.VMEM((tm, D_FF), BF16)],
        compiler_params=_params("parallel"),
        name="ffn",
    )(x, gain, wg, wu, wd)


def _rotate_every_two(t, even_lane):
    n = t.shape[-1]
    nxt = pltpu.roll(t, n - 1, 1)
    prv = pltpu.roll(t, 1, 1)
    return jnp.where(even_lane, -nxt, prv)


def _proj_kernel(x_ref, gain_ref, w_ref, cos_ref, sin_ref, poolw_ref, poolscale_ref,
                 rq_ref, rk_ref, rv_ref, sg_ref, ypool_ref, mqt_ref, mk_ref, mvt_ref,
                 halo_ref):
    tm = x_ref.shape[1]
    st = pl.program_id(1)
    h = _rms(x_ref[0], gain_ref[...]).astype(BF16)

    def cols(lo, width):
        return _dot(h, w_ref[:, lo:lo + width])

    cos = cos_ref[...]
    sin = sin_ref[...]
    even_lane = (lax.broadcasted_iota(jnp.int32, (tm, RET_QK), 1) % 2) == 0
    q = cols(0, RET_QK)
    rq_ref[0] = (q * cos + _rotate_every_two(q, even_lane) * sin).astype(BF16)
    k = cols(RET_QK, RET_QK)
    k = (k * cos + _rotate_every_two(k, even_lane) * sin) * (RET_QK_DIM ** -0.5)
    rk_ref[0] = k.astype(BF16)
    rv_ref[0] = cols(2 * RET_QK, RET_WIDTH).astype(BF16)
    g = cols(2 * RET_QK + RET_WIDTH, RET_WIDTH)
    sg_ref[0] = (g * jax.nn.sigmoid(g)).astype(BF16)

    off = 2 * RET_QK + 2 * RET_WIDTH
    u = cols(off, POOL_WIDTH)

    @pl.when(st == 0)
    def _():
        halo_ref[...] = jnp.zeros_like(halo_ref)

    ext = jnp.concatenate([halo_ref[...], u], axis=0)
    halo_ref[...] = u[tm - POOL_HALO:, :]
    lane = lax.broadcasted_iota(jnp.int32, (tm, POOL_WIDTH), 1)
    group = lane // POOL_GROUP_DIM
    wsum = None
    win_of_lane = None
    acc = ext
    for gi, win in enumerate(POOL_WINDOWS):
        acc = acc + pltpu.roll(acc, win // 2, 0)
        cur = acc[POOL_HALO:, :]
        wsum = cur if wsum is None else jnp.where(group == gi, cur, wsum)
        wl = jnp.full((tm, POOL_WIDTH), float(win), F32)
        win_of_lane = wl if win_of_lane is None else jnp.where(group == gi, wl, win_of_lane)
    pos = (st * tm + lax.broadcasted_iota(jnp.int32, (tm, POOL_WIDTH), 0) + 1).astype(F32)
    cnt = jnp.minimum(pos, win_of_lane)
    pooled = (wsum / cnt - u).astype(BF16)
    ypool_ref[0] = (_dot(pooled, poolw_ref[...]) * poolscale_ref[...]).astype(BF16)

    off += POOL_WIDTH
    mq = cols(off, MOBA_WIDTH) * (MOBA_HEAD_DIM ** -0.5 * LOG2_E)
    mqt_ref[0] = mq.T.astype(BF16)
    mk_ref[0] = cols(off + MOBA_WIDTH, MOBA_WIDTH).astype(BF16)
    mv = cols(off + 2 * MOBA_WIDTH, MOBA_WIDTH)
    ones = jnp.ones((MOBA_VT_ROWS - MOBA_HEAD_DIM, MOBA_BLOCK), F32)
    for j in range(tm // MOBA_BLOCK):
        mvt = mv[j * MOBA_BLOCK:(j + 1) * MOBA_BLOCK, :].T
        pieces = []
        for hd in range(MOBA_HEADS):
            pieces += [mvt[hd * MOBA_HEAD_DIM:(hd + 1) * MOBA_HEAD_DIM, :], ones]
        mvt_ref[0, j] = jnp.concatenate(pieces, axis=0).astype(BF16)


def _proj(x, gain, w_in, cos, sin, pool_w, pool_scale, layer):
    b, s, _ = x.shape
    tm = TOKEN_TILE
    in_cols = w_in.shape[-1]
    nb = s // MOBA_BLOCK
    bpt = tm // MOBA_BLOCK
    tok = lambda width: pl.BlockSpec((1, tm, width), lambda bi, si: (bi, si, 0))
    out_shape = [
        jax.ShapeDtypeStruct((b, s, RET_QK), BF16),
        jax.ShapeDtypeStruct((b, s, RET_QK), BF16),
        jax.ShapeDtypeStruct((b, s, RET_WIDTH), BF16),
        jax.ShapeDtypeStruct((b, s, RET_WIDTH), BF16),
        jax.ShapeDtypeStruct((b, s, POOL_WIDTH), BF16),
        jax.ShapeDtypeStruct((b, MOBA_WIDTH, s), BF16),
        jax.ShapeDtypeStruct((b, s, MOBA_WIDTH), BF16),
        jax.ShapeDtypeStruct((b, nb, MOBA_HEADS * MOBA_VT_ROWS, MOBA_BLOCK), BF16),
    ]
    out_specs = [
        tok(RET_QK), tok(RET_QK), tok(RET_WIDTH), tok(RET_WIDTH), tok(POOL_WIDTH),
        pl.BlockSpec((1, MOBA_WIDTH, tm), lambda bi, si: (bi, 0, si)),
        tok(MOBA_WIDTH),
        pl.BlockSpec((1, bpt, MOBA_HEADS * MOBA_VT_ROWS, MOBA_BLOCK), lambda bi, si: (bi, si, 0, 0)),
    ]
    return pl.pallas_call(
        _proj_kernel,
        grid=(b, s // tm),
        in_specs=[
            pl.BlockSpec((1, tm, D_MODEL), lambda bi, si: (bi, si, 0)),
            _resident((None, 1, D_MODEL), lambda bi, si: (layer, 0, 0)),
            _resident((None, D_MODEL, in_cols), lambda bi, si: (layer, 0, 0)),
            pl.BlockSpec((tm, RET_QK), lambda bi, si: (si, 0)),
            pl.BlockSpec((tm, RET_QK), lambda bi, si: (si, 0)),
            _resident((None, POOL_WIDTH, POOL_WIDTH), lambda bi, si: (layer, 0, 0)),
            _resident((None, 1, POOL_WIDTH), lambda bi, si: (layer, 0, 0)),
        ],
        out_specs=out_specs,
        out_shape=out_shape,
        scratch_shapes=[pltpu.VMEM((POOL_HALO, POOL_WIDTH), F32)],
        compiler_params=_params("parallel", "arbitrary"),
        name="proj",
    )(x, gain, w_in, cos, sin, pool_w, pool_scale)


def _retention_kernel(q_ref, k_ref, v_ref, sg_ref, dmask_ref, xi_ref, zeta_ref, decay_ref,
                      ondiag_ref, o_ref, state_ref):
    @pl.when(pl.program_id(1) == 0)
    def _():
        state_ref[...] = jnp.zeros_like(state_ref)

    q = q_ref[0]
    k = k_ref[0]
    v = v_ref[0]
    chunk = q.shape[0]
    state = state_ref[...]
    cross = _dot(q, state.astype(BF16)) * xi_ref[...]
    lane_head = lax.broadcasted_iota(jnp.int32, (chunk, RET_QK), 1) // RET_QK_DIM
    for hd in range(RET_HEADS):
        vcols = slice(hd * RET_V_DIM, (hd + 1) * RET_V_DIM)
        qh = jnp.where(lane_head == hd, q, jnp.zeros_like(q))
        inner = (_dot_nt(qh, k) * dmask_ref[hd]).astype(BF16)
        o = _dot(inner, v[:, vcols]) + cross[:, vcols]
        mu = jnp.mean(o, axis=-1, keepdims=True)
        var = jnp.mean(jnp.square(o - mu), axis=-1, keepdims=True)
        o = (o - mu) * lax.rsqrt(var + EPS)
        o_ref[0, :, vcols] = (o * sg_ref[0, :, vcols].astype(F32)).astype(BF16)

    kz = (k.astype(F32) * zeta_ref[...]).astype(BF16)
    state_ref[...] = decay_ref[...] * state + ondiag_ref[...] * _dot_tn(kz, v)


def _retention(rq, rk, rv, sg, dmask, xi, zeta, decay, on_diag):
    b, s, _ = rq.shape
    c = RET_CHUNK
    tok = lambda width: pl.BlockSpec((1, c, width), lambda bi, ci: (bi, ci, 0))
    const = lambda a: _resident(a.shape, lambda bi, ci: (0,) * a.ndim)
    return pl.pallas_call(
        _retention_kernel,
        grid=(b, s // c),
        in_specs=[tok(RET_QK), tok(RET_QK), tok(RET_WIDTH), tok(RET_WIDTH),
                  const(dmask), const(xi), const(zeta), const(decay), const(on_diag)],
        out_specs=tok(RET_WIDTH),
        out_shape=jax.ShapeDtypeStruct((b, s, RET_WIDTH), BF16),
        scratch_shapes=[pltpu.VMEM((RET_QK, RET_WIDTH), F32)],
        compiler_params=_params("parallel", "arbitrary"),
        name="retention",
    )(rq, rk, rv, sg, dmask, xi, zeta, decay, on_diag)


def _retention_tables():
    c = RET_CHUNK
    log_gamma = jnp.log(1.0 - jnp.power(2.0, -5.0 - jnp.arange(RET_HEADS, dtype=F32)))
    idx = jnp.arange(c, dtype=F32)
    diff = idx[:, None] - idx[None, :]
    dmask = jnp.where(diff >= 0, jnp.exp(log_gamma[:, None, None] * jnp.maximum(diff, 0.0)), 0.0)
    xi = jnp.exp(log_gamma[:, None] * (idx + 1.0))
    zeta = jnp.exp(log_gamma[:, None] * (c - 1.0 - idx))
    g_chunk = jnp.exp(log_gamma * c)
    xi_t = jnp.repeat(xi.T, RET_V_DIM, axis=1)
    zeta_t = jnp.repeat(zeta.T, RET_QK_DIM, axis=1)
    head_of_row = np.arange(RET_QK) // RET_QK_DIM
    head_of_col = np.arange(RET_WIDTH) // RET_V_DIM
    on_diag = jnp.asarray(head_of_row[:, None] == head_of_col[None, :], dtype=F32)
    decay = on_diag * jnp.repeat(g_chunk, RET_QK_DIM)[:, None]
    return dmask, xi_t, zeta_t, decay, on_diag


def _rotation_tables(seq):
    pos = jnp.arange(seq, dtype=F32)
    angle = 1.0 / (10000.0 ** jnp.linspace(0.0, 1.0, RET_QK_DIM // 2, dtype=F32))
    angle = jnp.repeat(angle, 2)
    ang = pos[:, None] * angle[None, :]
    return jnp.tile(jnp.cos(ang), (1, RET_HEADS)), jnp.tile(jnp.sin(ang), (1, RET_HEADS))


def _moba_kernel(qt_ref, k_ref, vt_ref, o_ref,
                 kmean_ref, qh_ref, pen_ref, m_ref, acc_ref, p_ref, s_even_ref, s_odd_ref):
    qi = pl.program_id(1)
    nb = kmean_ref.shape[0]
    blk = MOBA_BLOCK
    tq = qt_ref.shape[2]
    heads = range(MOBA_HEADS)
    vt_rows = [slice(hd * MOBA_VT_ROWS, (hd + 1) * MOBA_VT_ROWS) for hd in heads]
    chunks = [slice(r, r + MOBA_ROW_CHUNK) for r in range(0, blk, MOBA_ROW_CHUNK)]

    @pl.when(qi == 0)
    def _():
        for j in range(nb):
            kb = k_ref[0, j * blk:(j + 1) * blk, :].astype(F32)
            kmean_ref[j:j + 1, :] = jnp.mean(kb, axis=0, keepdims=True)

    qt = qt_ref[0]
    row_head = lax.broadcasted_iota(jnp.int32, (MOBA_WIDTH, tq), 0) // MOBA_HEAD_DIM
    for hd in heads:
        qh_ref[hd] = jnp.where(row_head == hd, qt, jnp.zeros_like(qt))

    kmean = kmean_ref[...].astype(BF16)
    blk_id = lax.broadcasted_iota(jnp.int32, (nb, tq), 0)
    past = blk_id < qi
    for hd in heads:
        gate = jnp.where(past, _dot(kmean, qh_ref[hd]), NEG_INF)
        chosen = jnp.zeros((nb, tq), jnp.bool_)
        for _ in range(MOBA_TOPK):
            best = jnp.max(gate, axis=0, keepdims=True)
            idx = jnp.min(jnp.where(gate == best, blk_id, nb), axis=0, keepdims=True)
            hit = blk_id == idx
            chosen = jnp.logical_or(chosen, jnp.logical_and(hit, past))
            gate = jnp.where(hit, NEG_INF, gate)
        pen_ref[hd] = jnp.where(chosen, 0.0, float("inf"))

    m_ref[...] = jnp.full(m_ref.shape, MOBA_M_INIT, F32)
    acc_ref[...] = jnp.zeros_like(acc_ref)

    def scores(j, s_ref):
        kj = k_ref[0, pl.ds(pl.multiple_of(j * blk, blk), blk), :]
        for hd in heads:
            s_ref[hd] = _dot(kj, qh_ref[hd])

    def attend(j, s_ref, own):
        for hd in heads:
            def chunk(c):
                s = s_ref[hd, c, :]
                if own:
                    key_pos = c.start + lax.broadcasted_iota(jnp.int32, s.shape, 0)
                    qry_pos = lax.broadcasted_iota(jnp.int32, s.shape, 1)
                    s = jnp.where(key_pos <= qry_pos, s, NEG_INF)
                return s

            top = chunk(chunks[0])
            for c in chunks[1:]:
                top = jnp.maximum(top, chunk(c))
            m_blk = jnp.max(top, axis=0, keepdims=True)
            m_old = m_ref[hd:hd + 1, :]
            if own:
                m_new = jnp.maximum(m_old, m_blk)
                shift = m_new
            else:
                pen = pen_ref[hd, pl.ds(j, 1), :]
                m_new = jnp.maximum(m_old, m_blk - pen)
                shift = m_new + pen
            m_ref[hd:hd + 1, :] = m_new
            for c in chunks:
                p_ref[hd, c, :] = jnp.exp2(chunk(c) - shift).astype(BF16)
            pv = _dot(vt_ref[0, j, vt_rows[hd], :], p_ref[hd])
            acc_ref[vt_rows[hd], :] = jnp.exp2(m_old - m_new) * acc_ref[vt_rows[hd], :] + pv

    scores(0, s_even_ref)

    def body(t, carry):
        j = 2 * t
        scores(j + 1, s_odd_ref)
        attend(j, s_even_ref, own=False)

        @pl.when(j + 1 < qi)
        def _():
            scores(j + 2, s_even_ref)
            attend(j + 1, s_odd_ref, own=False)
        return carry

    lax.fori_loop(0, (qi + 1) // 2, body, 0)

    @pl.when(qi % 2 == 0)
    def _():
        attend(qi, s_even_ref, own=True)

    @pl.when(qi % 2 == 1)
    def _():
        attend(qi, s_odd_ref, own=True)

    out = []
    for hd in heads:
        base = hd * MOBA_VT_ROWS
        denom = acc_ref[base + MOBA_HEAD_DIM:base + MOBA_HEAD_DIM + 1, :]
        out.append(acc_ref[base:base + MOBA_HEAD_DIM, :] / denom)
    o_ref[0] = jnp.concatenate(out, axis=0).T.astype(BF16)


def _moba(mqt, mk, mvt):
    b, s, _ = mk.shape
    nb = s // MOBA_BLOCK
    tq = MOBA_BLOCK
    vt_rows = MOBA_HEADS * MOBA_VT_ROWS
    scores_buf = pltpu.VMEM((MOBA_HEADS, MOBA_BLOCK, tq), F32)
    return pl.pallas_call(
        _moba_kernel,
        grid=(b, nb),
        in_specs=[
            pl.BlockSpec((1, MOBA_WIDTH, tq), lambda bi, qi: (bi, 0, qi)),
            pl.BlockSpec((1, s, MOBA_WIDTH), lambda bi, qi: (bi, 0, 0)),
            pl.BlockSpec((1, nb, vt_rows, MOBA_BLOCK), lambda bi, qi: (bi, 0, 0, 0)),
        ],
        out_specs=pl.BlockSpec((1, tq, MOBA_WIDTH), lambda bi, qi: (bi, qi, 0)),
        out_shape=jax.ShapeDtypeStruct((b, s, MOBA_WIDTH), BF16),
        scratch_shapes=[
            pltpu.VMEM((nb, MOBA_WIDTH), F32),
            pltpu.VMEM((MOBA_HEADS, MOBA_WIDTH, tq), BF16),
            pltpu.VMEM((MOBA_HEADS, nb, tq), F32),
            pltpu.VMEM((8, tq), F32),
            pltpu.VMEM((vt_rows, tq), F32),
            pltpu.VMEM((MOBA_HEADS, MOBA_BLOCK, tq), BF16),
            scores_buf, scores_buf,
        ],
        compiler_params=_params("parallel", "arbitrary"),
        name="moba",
    )(mqt, mk, mvt)


def _outproj_kernel(x_ref, yr_ref, yp_ref, ym_ref, w_ref, o_ref):
    acc = _dot(yr_ref[...], w_ref[:RET_WIDTH, :])
    acc += _dot(yp_ref[...], w_ref[RET_WIDTH:RET_WIDTH + POOL_WIDTH, :])
    acc += _dot(ym_ref[...], w_ref[RET_WIDTH + POOL_WIDTH:, :])
    o_ref[...] = x_ref[...] + acc


def _outproj(x, y_ret, y_pool, y_moba, w_out, layer):
    tokens = x.shape[0]
    tm = TOKEN_TILE
    tok = lambda width: pl.BlockSpec((tm, width), lambda i: (i, 0))
    return pl.pallas_call(
        _outproj_kernel,
        grid=(tokens // tm,),
        in_specs=[tok(D_MODEL), tok(RET_WIDTH), tok(POOL_WIDTH), tok(MOBA_WIDTH),
                  _resident((None, D_MODEL, D_MODEL), lambda i: (layer, 0, 0))],
        out_specs=tok(D_MODEL),
        out_shape=jax.ShapeDtypeStruct(x.shape, F32),
        compiler_params=_params("parallel"),
        name="outproj",
    )(x, y_ret, y_pool, y_moba, w_out)


def _ple_kernel(x_ref, p_ref, gain_ref, wgate_ref, wproj_ref, final_gain_ref, o_ref, *, final):
    x = x_ref[...]
    h = _rms(x, gain_ref[...]).astype(BF16)
    gate = jax.nn.sigmoid(_dot(h, wgate_ref[...]))
    x = x + gate * _dot(p_ref[...].astype(BF16), wproj_ref[...])
    if final:
        x = _rms(x, final_gain_ref[...])
    o_ref[...] = x


def _ple(x, p, gain, w_gate, w_proj, final_gain, layer, final):
    tokens = x.shape[0]
    tm = TOKEN_TILE
    return pl.pallas_call(
        functools.partial(_ple_kernel, final=final),
        grid=(tokens // tm,),
        in_specs=[
            pl.BlockSpec((tm, D_MODEL), lambda i: (i, 0)),
            pl.BlockSpec((None, tm, PLE_DIM), lambda i: (layer, i, 0)),
            _resident((None, 1, D_MODEL), lambda i: (layer, 0, 0)),
            _resident((None, D_MODEL, D_MODEL), lambda i: (layer, 0, 0)),
            _resident((None, PLE_DIM, D_MODEL), lambda i: (layer, 0, 0)),
            _resident((1, D_MODEL), lambda i: (0, 0)),
        ],
        out_specs=pl.BlockSpec((tm, D_MODEL), lambda i: (i, 0)),
        out_shape=jax.ShapeDtypeStruct(x.shape, F32),
        compiler_params=_params("parallel"),
        name="ple",
    )(x, p, gain, w_gate, w_proj, final_gain)


def _block_diag(w):
    depth, groups, d, _ = w.shape
    eye = jnp.eye(groups, dtype=w.dtype)
    return jnp.einsum("lgcd,gh->lgchd", w, eye).reshape(depth, groups * d, groups * d)


def kernel(x, p, norm_ffn1, ffn1_w_gate, ffn1_w_up, ffn1_w_down, norm_mix, w_in, pool_w, pool_scale, w_out, norm_ffn2, ffn2_w_gate, ffn2_w_up, ffn2_w_down, norm_ple, ple_w_gate, ple_w_proj, norm_final):
    b, s, d = x.shape
    depth = w_in.shape[0]
    tokens = b * s
    bf = lambda w: w.astype(BF16)
    row = lambda g: g.reshape(g.shape[0], 1, g.shape[1])

    ffn1 = (row(norm_ffn1), bf(ffn1_w_gate), bf(ffn1_w_up), bf(ffn1_w_down))
    ffn2 = (row(norm_ffn2), bf(ffn2_w_gate), bf(ffn2_w_up), bf(ffn2_w_down))
    w_in_b, w_out_b = bf(w_in), bf(w_out)
    pool_w_b = bf(_block_diag(pool_w))
    ple_gate_b, ple_proj_b = bf(ple_w_gate), bf(ple_w_proj)
    norm_mix_r, norm_ple_r, pool_scale_r = row(norm_mix), row(norm_ple), row(pool_scale)
    final_gain = norm_final.reshape(1, d)
    p_flat = p.reshape(depth, tokens, PLE_DIM)
    cos, sin = _rotation_tables(s)
    ret_tables = _retention_tables()

    xf = x.reshape(tokens, d)
    for i in range(depth):
        xf = _ffn(xf, *ffn1, i)
        rq, rk, rv, sg, y_pool, mqt, mk, mvt = _proj(
            xf.reshape(b, s, d), norm_mix_r, w_in_b, cos, sin, pool_w_b, pool_scale_r, i)
        y_ret = _retention(rq, rk, rv, sg, *ret_tables)
        y_moba = _moba(mqt, mk, mvt)
        xf = _outproj(xf, y_ret.reshape(tokens, RET_WIDTH), y_pool.reshape(tokens, POOL_WIDTH),
                      y_moba.reshape(tokens, MOBA_WIDTH), w_out_b, i)
        xf = _ffn(xf, *ffn2, i)
        xf = _ple(xf, p_flat, norm_ple_r, ple_gate_b, ple_proj_b, final_gain, i, i == depth - 1)
    return xf.reshape(b, s, d)
```

```python
import functools

import jax
import jax.numpy as jnp
import numpy as np
from jax import lax
from jax.experimental import pallas as pl
from jax.experimental.pallas import tpu as pltpu

D_MODEL = 1024
D_FF = 2816
RET_HEADS = 4
RET_QK_DIM = 64
RET_V_DIM = 128
RET_CHUNK = 256
RET_QK = RET_HEADS * RET_QK_DIM
RET_WIDTH = RET_HEADS * RET_V_DIM
POOL_GROUPS = 4
POOL_WINDOWS = (2, 4, 8, 16)
POOL_GROUP_DIM = 64
POOL_WIDTH = POOL_GROUPS * POOL_GROUP_DIM
POOL_HALO = 16
assert POOL_WINDOWS == tuple(2 ** (i + 1) for i in range(POOL_GROUPS)) and POOL_HALO == POOL_WINDOWS[-1]
MOBA_HEADS = 4
MOBA_HEAD_DIM = 64
MOBA_WIDTH = MOBA_HEADS * MOBA_HEAD_DIM
MOBA_BLOCK = 256
MOBA_TOPK = 3
MOBA_VT_ROWS = MOBA_HEAD_DIM + 16
MOBA_ROW_CHUNK = 64
MOBA_M_INIT = -(2.0 ** 100)
PLE_DIM = 256
EPS = 1e-6

FF_CHUNK = 256
TOKEN_TILE = 512
VMEM_LIMIT = 56 * 1024 * 1024

BF16 = jnp.bfloat16
F32 = jnp.float32
NEG_INF = float("-inf")
LOG2_E = 1.4426950408889634


def _params(*semantics):
    return pltpu.CompilerParams(dimension_semantics=semantics, vmem_limit_bytes=VMEM_LIMIT)


def _resident(shape, index_map):
    return pl.BlockSpec(shape, index_map, pipeline_mode=pl.Buffered(1))


def _rms(x, gain):
    return x * lax.rsqrt(jnp.mean(x * x, axis=-1, keepdims=True) + EPS) * gain


def _dot(a, b):
    return jnp.dot(a, b, preferred_element_type=F32)


def _dot_nt(a, b):
    return lax.dot_general(a, b, (((1,), (1,)), ((), ())), preferred_element_type=F32)


def _dot_tn(a, b):
    return lax.dot_general(a, b, (((0,), (0,)), ((), ())), preferred_element_type=F32)


def _swiglu_half_step(x, gain, wg_ref, wu_ref, wd_ref, act_ref):
    h = _rms(x, gain).astype(BF16)
    for c in range(D_FF // FF_CHUNK):
        cols = slice(c * FF_CHUNK, (c + 1) * FF_CHUNK)
        gate = _dot(h, wg_ref[:, cols])
        up = _dot(h, wu_ref[:, cols])
        act_ref[:, cols] = (gate * jax.nn.sigmoid(gate) * up).astype(BF16)
    return x + 0.5 * _dot(act_ref[...], wd_ref[...])


def _ffn_kernel(x_ref, gain_ref, wg_ref, wu_ref, wd_ref, o_ref, act_ref):
    o_ref[...] = _swiglu_half_step(x_ref[...], gain_ref[...], wg_ref, wu_ref, wd_ref, act_ref)


def _ffn(x, gain, wg, wu, wd, layer):
    tokens = x.shape[0]
    tm = TOKEN_TILE
    return pl.pallas_call(
        _ffn_kernel,
        grid=(tokens // tm,),
        in_specs=[
            pl.BlockSpec((tm, D_MODEL), lambda i: (i, 0)),
            _resident((None, 1, D_MODEL), lambda i: (layer, 0, 0)),
            _resident((None, D_MODEL, D_FF), lambda i: (layer, 0, 0)),
            _resident((None, D_MODEL, D_FF), lambda i: (layer, 0, 0)),
            _resident((None, D_FF, D_MODEL), lambda i: (layer, 0, 0)),
        ],
        out_specs=pl.BlockSpec((tm, D_MODEL), lambda i: (i, 0)),
        out_shape=jax.ShapeDtypeStruct(x.shape, F32),
        scratch_shapes=[pltpu.VMEM((tm, D_FF), BF16)],
        compiler_params=_params("parallel"),
        name="ffn",
    )(x, gain, wg, wu, wd)


def _rotate_every_two(t, even_lane):
    n = t.shape[-1]
    nxt = pltpu.roll(t, n - 1, 1)
    prv = pltpu.roll(t, 1, 1)
    return jnp.where(even_lane, -nxt, prv)


def _proj_kernel(x_ref, gain_ref, w_ref, cos_ref, sin_ref, poolw_ref, poolscale_ref,
                 rq_ref, rk_ref, rv_ref, sg_ref, ypool_ref, mqt_ref, mk_ref, mvt_ref,
                 halo_ref):
    tm = x_ref.shape[1]
    st = pl.program_id(1)
    h = _rms(x_ref[0], gain_ref[...]).astype(BF16)

    def cols(lo, width):
        return _dot(h, w_ref[:, lo:lo + width])

    cos = cos_ref[...]
    sin = sin_ref[...]
    even_lane = (lax.broadcasted_iota(jnp.int32, (tm, RET_QK), 1) % 2) == 0
    q = cols(0, RET_QK)
    rq_ref[0] = (q * cos + _rotate_every_two(q, even_lane) * sin).astype(BF16)
    k = cols(RET_QK, RET_QK)
    k = (k * cos + _rotate_every_two(k, even_lane) * sin) * (RET_QK_DIM ** -0.5)
    rk_ref[0] = k.astype(BF16)
    rv_ref[0] = cols(2 * RET_QK, RET_WIDTH).astype(BF16)
    g = cols(2 * RET_QK + RET_WIDTH, RET_WIDTH)
    sg_ref[0] = (g * jax.nn.sigmoid(g)).astype(BF16)

    off = 2 * RET_QK + 2 * RET_WIDTH
    u = cols(off, POOL_WIDTH)

    @pl.when(st == 0)
    def _():
        halo_ref[...] = jnp.zeros_like(halo_ref)

    ext = jnp.concatenate([halo_ref[...], u], axis=0)
    halo_ref[...] = u[tm - POOL_HALO:, :]
    group = lax.broadcasted_iota(jnp.int32, (tm, POOL_WIDTH), 1) // POOL_GROUP_DIM
    group_row = lax.broadcasted_iota(jnp.int32, (1, POOL_WIDTH), 1) // POOL_GROUP_DIM
    wsum = None
    window = None
    acc = ext
    for gi, win in enumerate(POOL_WINDOWS):
        acc = acc + pltpu.roll(acc, win // 2, 0)
        cur = acc[POOL_HALO:, :]
        wsum = cur if wsum is None else jnp.where(group == gi, cur, wsum)
        wl = jnp.full((1, POOL_WIDTH), float(win), F32)
        window = wl if window is None else jnp.where(group_row == gi, wl, window)

    def emit_pool(pooled):
        y = _dot(pooled.astype(BF16), poolw_ref[...]) * poolscale_ref[...]
        ypool_ref[0] = y.astype(BF16)

    @pl.when(st == 0)
    def _():
        pos = (lax.broadcasted_iota(jnp.int32, (tm, POOL_WIDTH), 0) + 1).astype(F32)
        emit_pool(wsum / jnp.minimum(pos, window) - u)

    @pl.when(st > 0)
    def _():
        emit_pool(wsum * (1.0 / window) - u)

    off += POOL_WIDTH
    mq = cols(off, MOBA_WIDTH) * (MOBA_HEAD_DIM ** -0.5 * LOG2_E)
    mqt_ref[0] = mq.T.astype(BF16)
    mk_ref[0] = cols(off + MOBA_WIDTH, MOBA_WIDTH).astype(BF16)
    mv = cols(off + 2 * MOBA_WIDTH, MOBA_WIDTH)
    ones = jnp.ones((MOBA_VT_ROWS - MOBA_HEAD_DIM, MOBA_BLOCK), F32)
    for j in range(tm // MOBA_BLOCK):
        mvt = mv[j * MOBA_BLOCK:(j + 1) * MOBA_BLOCK, :].T
        pieces = []
        for hd in range(MOBA_HEADS):
            pieces += [mvt[hd * MOBA_HEAD_DIM:(hd + 1) * MOBA_HEAD_DIM, :], ones]
        mvt_ref[0, j] = jnp.concatenate(pieces, axis=0).astype(BF16)


def _proj(x, gain, w_in, cos, sin, pool_w, pool_scale, layer):
    b, s, _ = x.shape
    tm = TOKEN_TILE
    in_cols = w_in.shape[-1]
    nb = s // MOBA_BLOCK
    bpt = tm // MOBA_BLOCK
    tok = lambda width: pl.BlockSpec((1, tm, width), lambda bi, si: (bi, si, 0))
    out_shape = [
        jax.ShapeDtypeStruct((b, s, RET_QK), BF16),
        jax.ShapeDtypeStruct((b, s, RET_QK), BF16),
        jax.ShapeDtypeStruct((b, s, RET_WIDTH), BF16),
        jax.ShapeDtypeStruct((b, s, RET_WIDTH), BF16),
        jax.ShapeDtypeStruct((b, s, POOL_WIDTH), BF16),
        jax.ShapeDtypeStruct((b, MOBA_WIDTH, s), BF16),
        jax.ShapeDtypeStruct((b, s, MOBA_WIDTH), BF16),
        jax.ShapeDtypeStruct((b, nb, MOBA_HEADS * MOBA_VT_ROWS, MOBA_BLOCK), BF16),
    ]
    out_specs = [
        tok(RET_QK), tok(RET_QK), tok(RET_WIDTH), tok(RET_WIDTH), tok(POOL_WIDTH),
        pl.BlockSpec((1, MOBA_WIDTH, tm), lambda bi, si: (bi, 0, si)),
        tok(MOBA_WIDTH),
        pl.BlockSpec((1, bpt, MOBA_HEADS * MOBA_VT_ROWS, MOBA_BLOCK), lambda bi, si: (bi, si, 0, 0)),
    ]
    return pl.pallas_call(
        _proj_kernel,
        grid=(b, s // tm),
        in_specs=[
            pl.BlockSpec((1, tm, D_MODEL), lambda bi, si: (bi, si, 0)),
            _resident((None, 1, D_MODEL), lambda bi, si: (layer, 0, 0)),
            _resident((None, D_MODEL, in_cols), lambda bi, si: (layer, 0, 0)),
            pl.BlockSpec((tm, RET_QK), lambda bi, si: (si, 0)),
            pl.BlockSpec((tm, RET_QK), lambda bi, si: (si, 0)),
            _resident((None, POOL_WIDTH, POOL_WIDTH), lambda bi, si: (layer, 0, 0)),
            _resident((None, 1, POOL_WIDTH), lambda bi, si: (layer, 0, 0)),
        ],
        out_specs=out_specs,
        out_shape=out_shape,
        scratch_shapes=[pltpu.VMEM((POOL_HALO, POOL_WIDTH), F32)],
        compiler_params=_params("parallel", "arbitrary"),
        name="proj",
    )(x, gain, w_in, cos, sin, pool_w, pool_scale)


def _retention_kernel(q_ref, k_ref, v_ref, sg_ref, dmask_ref, xi_ref, zeta_ref, decay_ref,
                      ondiag_ref, o_ref, state_ref):
    @pl.when(pl.program_id(1) == 0)
    def _():
        state_ref[...] = jnp.zeros_like(state_ref)

    q = q_ref[0]
    k = k_ref[0]
    v = v_ref[0]
    chunk = q.shape[0]
    state = state_ref[...]
    cross = _dot(q, state.astype(BF16)) * xi_ref[...]
    lane_head = lax.broadcasted_iota(jnp.int32, (chunk, RET_QK), 1) // RET_QK_DIM
    for hd in range(RET_HEADS):
        vcols = slice(hd * RET_V_DIM, (hd + 1) * RET_V_DIM)
        qh = jnp.where(lane_head == hd, q, jnp.zeros_like(q))
        inner = (_dot_nt(qh, k) * dmask_ref[hd]).astype(BF16)
        o = _dot(inner, v[:, vcols]) + cross[:, vcols]
        mu = jnp.mean(o, axis=-1, keepdims=True)
        var = jnp.mean(jnp.square(o - mu), axis=-1, keepdims=True)
        o = (o - mu) * lax.rsqrt(var + EPS)
        o_ref[0, :, vcols] = (o * sg_ref[0, :, vcols].astype(F32)).astype(BF16)

    kz = (k.astype(F32) * zeta_ref[...]).astype(BF16)
    state_ref[...] = decay_ref[...] * state + ondiag_ref[...] * _dot_tn(kz, v)


def _retention(rq, rk, rv, sg, dmask, xi, zeta, decay, on_diag):
    b, s, _ = rq.shape
    c = RET_CHUNK
    tok = lambda width: pl.BlockSpec((1, c, width), lambda bi, ci: (bi, ci, 0))
    const = lambda a: _resident(a.shape, lambda bi, ci: (0,) * a.ndim)
    return pl.pallas_call(
        _retention_kernel,
        grid=(b, s // c),
        in_specs=[tok(RET_QK), tok(RET_QK), tok(RET_WIDTH), tok(RET_WIDTH),
                  const(dmask), const(xi), const(zeta), const(decay), const(on_diag)],
        out_specs=tok(RET_WIDTH),
        out_shape=jax.ShapeDtypeStruct((b, s, RET_WIDTH), BF16),
        scratch_shapes=[pltpu.VMEM((RET_QK, RET_WIDTH), F32)],
        compiler_params=_params("parallel", "arbitrary"),
        name="retention",
    )(rq, rk, rv, sg, dmask, xi, zeta, decay, on_diag)


def _retention_tables():
    c = RET_CHUNK
    log_gamma = jnp.log(1.0 - jnp.power(2.0, -5.0 - jnp.arange(RET_HEADS, dtype=F32)))
    idx = jnp.arange(c, dtype=F32)
    diff = idx[:, None] - idx[None, :]
    dmask = jnp.where(diff >= 0, jnp.exp(log_gamma[:, None, None] * jnp.maximum(diff, 0.0)), 0.0)
    xi = jnp.exp(log_gamma[:, None] * (idx + 1.0))
    zeta = jnp.exp(log_gamma[:, None] * (c - 1.0 - idx))
    g_chunk = jnp.exp(log_gamma * c)
    xi_t = jnp.repeat(xi.T, RET_V_DIM, axis=1)
    zeta_t = jnp.repeat(zeta.T, RET_QK_DIM, axis=1)
    head_of_row = np.arange(RET_QK) // RET_QK_DIM
    head_of_col = np.arange(RET_WIDTH) // RET_V_DIM
    on_diag = jnp.asarray(head_of_row[:, None] == head_of_col[None, :], dtype=F32)
    decay = on_diag * jnp.repeat(g_chunk, RET_QK_DIM)[:, None]
    return dmask, xi_t, zeta_t, decay, on_diag


def _rotation_tables(seq):
    pos = jnp.arange(seq, dtype=F32)
    angle = 1.0 / (10000.0 ** jnp.linspace(0.0, 1.0, RET_QK_DIM // 2, dtype=F32))
    angle = jnp.repeat(angle, 2)
    ang = pos[:, None] * angle[None, :]
    return jnp.tile(jnp.cos(ang), (1, RET_HEADS)), jnp.tile(jnp.sin(ang), (1, RET_HEADS))


def _moba_kernel(qt_ref, k_ref, vt_ref, o_ref,
                 kmean_ref, qh_ref, pen_ref, m_ref, acc_ref, p_ref,
                 s_even_ref, s_odd_ref, top_even_ref, top_odd_ref):
    qi = pl.program_id(1)
    nb = kmean_ref.shape[0]
    blk = MOBA_BLOCK
    tq = qt_ref.shape[2]
    heads = range(MOBA_HEADS)
    vt_rows = [slice(hd * MOBA_VT_ROWS, (hd + 1) * MOBA_VT_ROWS) for hd in heads]
    chunks = [slice(r, r + MOBA_ROW_CHUNK) for r in range(0, blk, MOBA_ROW_CHUNK)]

    @pl.when(qi == 0)
    def _():
        for j in range(nb):
            kb = k_ref[0, j * blk:(j + 1) * blk, :].astype(F32)
            kmean_ref[j:j + 1, :] = jnp.mean(kb, axis=0, keepdims=True)

    qt = qt_ref[0]
    row_head = lax.broadcasted_iota(jnp.int32, (MOBA_WIDTH, tq), 0) // MOBA_HEAD_DIM
    for hd in heads:
        qh_ref[hd] = jnp.where(row_head == hd, qt, jnp.zeros_like(qt))

    kmean = kmean_ref[...].astype(BF16)
    blk_id = lax.broadcasted_iota(jnp.int32, (nb, tq), 0)
    past = blk_id < qi
    for hd in heads:
        gate = jnp.where(past, _dot(kmean, qh_ref[hd]), NEG_INF)
        chosen = jnp.zeros((nb, tq), jnp.bool_)
        for _ in range(MOBA_TOPK):
            best = jnp.max(gate, axis=0, keepdims=True)
            idx = jnp.min(jnp.where(gate == best, blk_id, nb), axis=0, keepdims=True)
            hit = blk_id == idx
            chosen = jnp.logical_or(chosen, jnp.logical_and(hit, past))
            gate = jnp.where(hit, NEG_INF, gate)
        pen_ref[hd] = jnp.where(chosen, 0.0, float("inf"))

    m_ref[...] = jnp.full(m_ref.shape, MOBA_M_INIT, F32)
    acc_ref[...] = jnp.zeros_like(acc_ref)

    def col_max(tiles):
        top = tiles[0]
        for t in tiles[1:]:
            top = jnp.maximum(top, t)
        return jnp.max(top, axis=0, keepdims=True)

    def scores(j, buf):
        s_ref, top_ref = buf
        kj = k_ref[0, pl.ds(pl.multiple_of(j * blk, blk), blk), :]
        for hd in heads:
            s = _dot(kj, qh_ref[hd])
            s_ref[hd] = s
            top_ref[hd:hd + 1, :] = col_max([s[c] for c in chunks])

    def attend(j, buf, own):
        s_ref, top_ref = buf
        for hd in heads:
            def chunk(c):
                s = s_ref[hd, c, :]
                if own:
                    key_pos = c.start + lax.broadcasted_iota(jnp.int32, s.shape, 0)
                    qry_pos = lax.broadcasted_iota(jnp.int32, s.shape, 1)
                    s = jnp.where(key_pos <= qry_pos, s, NEG_INF)
                return s

            m_old = m_ref[hd:hd + 1, :]
            if own:
                m_new = jnp.maximum(m_old, col_max([chunk(c) for c in chunks]))
                shift = m_new
            else:
                pen = pen_ref[hd, pl.ds(j, 1), :]
                m_new = jnp.maximum(m_old, top_ref[hd:hd + 1, :] - pen)
                shift = m_new + pen
            m_ref[hd:hd + 1, :] = m_new
            for c in chunks:
                p_ref[hd, c, :] = jnp.exp2(chunk(c) - shift).astype(BF16)
            pv = _dot(vt_ref[0, j, vt_rows[hd], :], p_ref[hd])
            acc_ref[vt_rows[hd], :] = jnp.exp2(m_old - m_new) * acc_ref[vt_rows[hd], :] + pv

    even = (s_even_ref, top_even_ref)
    odd = (s_odd_ref, top_odd_ref)
    scores(0, even)

    def body(t, carry):
        j = 2 * t
        scores(j + 1, odd)
        attend(j, even, own=False)
        scores(jnp.minimum(j + 2, qi), even)
        attend(j + 1, odd, own=False)
        return carry

    lax.fori_loop(0, (qi + 1) // 2, body, 0)
    attend(qi, even, own=True)

    out = []
    for hd in heads:
        base = hd * MOBA_VT_ROWS
        denom = acc_ref[base + MOBA_HEAD_DIM:base + MOBA_HEAD_DIM + 1, :]
        out.append(acc_ref[base:base + MOBA_HEAD_DIM, :] / denom)
    o_ref[0] = jnp.concatenate(out, axis=0).T.astype(BF16)


def _moba(mqt, mk, mvt):
    b, s, _ = mk.shape
    nb = s // MOBA_BLOCK
    tq = MOBA_BLOCK
    vt_rows = MOBA_HEADS * MOBA_VT_ROWS
    scores_buf = pltpu.VMEM((MOBA_HEADS, MOBA_BLOCK, tq), F32)
    head_rows_buf = pltpu.VMEM((8, tq), F32)
    return pl.pallas_call(
        _moba_kernel,
        grid=(b, nb),
        in_specs=[
            pl.BlockSpec((1, MOBA_WIDTH, tq), lambda bi, qi: (bi, 0, qi)),
            pl.BlockSpec((1, s, MOBA_WIDTH), lambda bi, qi: (bi, 0, 0)),
            pl.BlockSpec((1, nb, vt_rows, MOBA_BLOCK), lambda bi, qi: (bi, 0, 0, 0)),
        ],
        out_specs=pl.BlockSpec((1, tq, MOBA_WIDTH), lambda bi, qi: (bi, qi, 0)),
        out_shape=jax.ShapeDtypeStruct((b, s, MOBA_WIDTH), BF16),
        scratch_shapes=[
            pltpu.VMEM((nb, MOBA_WIDTH), F32),
            pltpu.VMEM((MOBA_HEADS, MOBA_WIDTH, tq), BF16),
            pltpu.VMEM((MOBA_HEADS, nb, tq), F32),
            head_rows_buf,
            pltpu.VMEM((vt_rows, tq), F32),
            pltpu.VMEM((MOBA_HEADS, MOBA_BLOCK, tq), BF16),
            scores_buf, scores_buf,
            head_rows_buf, head_rows_buf,
        ],
        compiler_params=_params("parallel", "arbitrary"),
        name="moba",
    )(mqt, mk, mvt)


def _tail_kernel(x_ref, yr_ref, yp_ref, ym_ref, p_ref, wout_ref,
                 ffn_gain_ref, wg_ref, wu_ref, wd_ref,
                 ple_gain_ref, ple_wgate_ref, ple_wproj_ref, final_gain_ref,
                 o_ref, act_ref, *, final):
    x = x_ref[...]
    x = x + _dot(yr_ref[...], wout_ref[:RET_WIDTH, :])
    x = x + _dot(yp_ref[...], wout_ref[RET_WIDTH:RET_WIDTH + POOL_WIDTH, :])
    x = x + _dot(ym_ref[...], wout_ref[RET_WIDTH + POOL_WIDTH:, :])
    x = _swiglu_half_step(x, ffn_gain_ref[...], wg_ref, wu_ref, wd_ref, act_ref)
    h = _rms(x, ple_gain_ref[...]).astype(BF16)
    gate = jax.nn.sigmoid(_dot(h, ple_wgate_ref[...]))
    x = x + gate * _dot(p_ref[...].astype(BF16), ple_wproj_ref[...])
    if final:
        x = _rms(x, final_gain_ref[...])
    o_ref[...] = x


def _tail(x, y_ret, y_pool, y_moba, p, w_out, ffn, ple_gain, ple_w_gate, ple_w_proj,
          final_gain, layer, final):
    tokens = x.shape[0]
    tm = TOKEN_TILE
    ffn_gain, wg, wu, wd = ffn
    tok = lambda width: pl.BlockSpec((tm, width), lambda i: (i, 0))
    per_layer = lambda *shape: _resident((None,) + shape, lambda i: (layer,) + (0,) * len(shape))
    return pl.pallas_call(
        functools.partial(_tail_kernel, final=final),
        grid=(tokens // tm,),
        in_specs=[
            tok(D_MODEL), tok(RET_WIDTH), tok(POOL_WIDTH), tok(MOBA_WIDTH),
            pl.BlockSpec((None, tm, PLE_DIM), lambda i: (layer, i, 0)),
            per_layer(D_MODEL, D_MODEL),
            per_layer(1, D_MODEL), per_layer(D_MODEL, D_FF), per_layer(D_MODEL, D_FF),
            per_layer(D_FF, D_MODEL),
            per_layer(1, D_MODEL), per_layer(D_MODEL, D_MODEL), per_layer(PLE_DIM, D_MODEL),
            _resident((1, D_MODEL), lambda i: (0, 0)),
        ],
        out_specs=tok(D_MODEL),
        out_shape=jax.ShapeDtypeStruct(x.shape, F32),
        scratch_shapes=[pltpu.VMEM((tm, D_FF), BF16)],
        compiler_params=_params("parallel"),
        name="tail",
    )(x, y_ret, y_pool, y_moba, p, w_out, ffn_gain, wg, wu, wd,
      ple_gain, ple_w_gate, ple_w_proj, final_gain)


def _block_diag(w):
    depth, groups, d, _ = w.shape
    eye = jnp.eye(groups, dtype=w.dtype)
    return jnp.einsum("lgcd,gh->lgchd", w, eye).reshape(depth, groups * d, groups * d)


def kernel(x, p, norm_ffn1, ffn1_w_gate, ffn1_w_up, ffn1_w_down, norm_mix, w_in, pool_w, pool_scale, w_out, norm_ffn2, ffn2_w_gate, ffn2_w_up, ffn2_w_down, norm_ple, ple_w_gate, ple_w_proj, norm_final):
    b, s, d = x.shape
    depth = w_in.shape[0]
    tokens = b * s
    bf = lambda w: w.astype(BF16)
    row = lambda g: g.reshape(g.shape[0], 1, g.shape[1])

    ffn1 = (row(norm_ffn1), bf(ffn1_w_gate), bf(ffn1_w_up), bf(ffn1_w_down))
    ffn2 = (row(norm_ffn2), bf(ffn2_w_gate), bf(ffn2_w_up), bf(ffn2_w_down))
    w_in_b, w_out_b = bf(w_in), bf(w_out)
    pool_w_b = bf(_block_diag(pool_w))
    ple_gate_b, ple_proj_b = bf(ple_w_gate), bf(ple_w_proj)
    norm_mix_r, norm_ple_r, pool_scale_r = row(norm_mix), row(norm_ple), row(pool_scale)
    final_gain = norm_final.reshape(1, d)
    p_flat = p.reshape(depth, tokens, PLE_DIM)
    cos, sin = _rotation_tables(s)
    ret_tables = _retention_tables()

    xf = x.reshape(tokens, d)
    for i in range(depth):
        xf = _ffn(xf, *ffn1, i)
        rq, rk, rv, sg, y_pool, mqt, mk, mvt = _proj(
            xf.reshape(b, s, d), norm_mix_r, w_in_b, cos, sin, pool_w_b, pool_scale_r, i)
        y_ret = _retention(rq, rk, rv, sg, *ret_tables)
        y_moba = _moba(mqt, mk, mvt)
        xf = _tail(xf, y_ret.reshape(tokens, RET_WIDTH), y_pool.reshape(tokens, POOL_WIDTH),
                   y_moba.reshape(tokens, MOBA_WIDTH), p_flat, w_out_b, ffn2,
                   norm_ple_r, ple_gate_b, ple_proj_b, final_gain, i, i == depth - 1)
    return xf.reshape(b, s, d)
```

```python
import functools

import jax
import jax.numpy as jnp
import numpy as np
from jax import lax
from jax.experimental import pallas as pl
from jax.experimental.pallas import tpu as pltpu

D_MODEL = 1024
D_FF = 2816
RET_HEADS = 4
RET_QK_DIM = 64
RET_V_DIM = 128
RET_CHUNK = 256
RET_QK = RET_HEADS * RET_QK_DIM
RET_WIDTH = RET_HEADS * RET_V_DIM
POOL_GROUPS = 4
POOL_WINDOWS = (2, 4, 8, 16)
POOL_GROUP_DIM = 64
POOL_WIDTH = POOL_GROUPS * POOL_GROUP_DIM
POOL_HALO = 16
assert POOL_WINDOWS == tuple(2 ** (i + 1) for i in range(POOL_GROUPS)) and POOL_HALO == POOL_WINDOWS[-1]
MOBA_HEADS = 4
MOBA_HEAD_DIM = 64
MOBA_WIDTH = MOBA_HEADS * MOBA_HEAD_DIM
MOBA_BLOCK = 256
MOBA_TOPK = 3
MOBA_VT_ROWS = MOBA_HEAD_DIM + 16
MOBA_ROW_CHUNK = 64
MOBA_M_INIT = -(2.0 ** 100)
PLE_DIM = 256
EPS = 1e-6

FF_CHUNK = 256
TOKEN_TILE = 512
VMEM_LIMIT = 56 * 1024 * 1024

BF16 = jnp.bfloat16
F32 = jnp.float32
NEG_INF = float("-inf")
LOG2_E = 1.4426950408889634


def _params(*semantics):
    return pltpu.CompilerParams(dimension_semantics=semantics, vmem_limit_bytes=VMEM_LIMIT)


def _resident(shape, index_map):
    return pl.BlockSpec(shape, index_map, pipeline_mode=pl.Buffered(1))


def _rms(x, gain):
    return x * lax.rsqrt(jnp.mean(x * x, axis=-1, keepdims=True) + EPS) * gain


def _dot(a, b):
    return jnp.dot(a, b, preferred_element_type=F32)


def _dot_nt(a, b):
    return lax.dot_general(a, b, (((1,), (1,)), ((), ())), preferred_element_type=F32)


def _dot_tn(a, b):
    return lax.dot_general(a, b, (((0,), (0,)), ((), ())), preferred_element_type=F32)


def _swiglu_half_step(x, gain, wg_ref, wu_ref, wd_ref, act_ref):
    h = _rms(x, gain).astype(BF16)
    for c in range(D_FF // FF_CHUNK):
        cols = slice(c * FF_CHUNK, (c + 1) * FF_CHUNK)
        gate = _dot(h, wg_ref[:, cols])
        up = _dot(h, wu_ref[:, cols])
        act_ref[:, cols] = (gate * jax.nn.sigmoid(gate) * up).astype(BF16)
    return x + 0.5 * _dot(act_ref[...], wd_ref[...])


def _ffn_kernel(x_ref, gain_ref, wg_ref, wu_ref, wd_ref, o_ref, act_ref):
    o_ref[...] = _swiglu_half_step(x_ref[...], gain_ref[...], wg_ref, wu_ref, wd_ref, act_ref)


def _ffn(x, gain, wg, wu, wd, layer):
    tokens = x.shape[0]
    tm = TOKEN_TILE
    return pl.pallas_call(
        _ffn_kernel,
        grid=(tokens // tm,),
        in_specs=[
            pl.BlockSpec((tm, D_MODEL), lambda i: (i, 0)),
            _resident((None, 1, D_MODEL), lambda i: (layer, 0, 0)),
            _resident((None, D_MODEL, D_FF), lambda i: (layer, 0, 0)),
            _resident((None, D_MODEL, D_FF), lambda i: (layer, 0, 0)),
            _resident((None, D_FF, D_MODEL), lambda i: (layer, 0, 0)),
        ],
        out_specs=pl.BlockSpec((tm, D_MODEL), lambda i: (i, 0)),
        out_shape=jax.ShapeDtypeStruct(x.shape, F32),
        scratch_shapes=[pltpu.VMEM((tm, D_FF), BF16)],
        compiler_params=_params("parallel"),
        name="ffn",
    )(x, gain, wg, wu, wd)


def _rotate_every_two(t, even_lane):
    n = t.shape[-1]
    nxt = pltpu.roll(t, n - 1, 1)
    prv = pltpu.roll(t, 1, 1)
    return jnp.where(even_lane, -nxt, prv)


def _proj_kernel(x_ref, gain_ref, w_ref, cos_ref, sin_ref, poolw_ref, poolscale_ref,
                 rq_ref, rk_ref, rv_ref, sg_ref, ypool_ref, mqt_ref, mk_ref, mvt_ref,
                 halo_ref):
    tm = x_ref.shape[1]
    st = pl.program_id(1)
    h = _rms(x_ref[0], gain_ref[...]).astype(BF16)

    def cols(lo, width):
        return _dot(h, w_ref[:, lo:lo + width])

    cos = cos_ref[...]
    sin = sin_ref[...]
    even_lane = (lax.broadcasted_iota(jnp.int32, (tm, RET_QK), 1) % 2) == 0
    q = cols(0, RET_QK)
    rq_ref[0] = (q * cos + _rotate_every_two(q, even_lane) * sin).astype(BF16)
    k = cols(RET_QK, RET_QK)
    k = (k * cos + _rotate_every_two(k, even_lane) * sin) * (RET_QK_DIM ** -0.5)
    rk_ref[0] = k.astype(BF16)
    rv_ref[0] = cols(2 * RET_QK, RET_WIDTH).astype(BF16)
    g = cols(2 * RET_QK + RET_WIDTH, RET_WIDTH)
    sg_ref[0] = (g * jax.nn.sigmoid(g)).astype(BF16)

    off = 2 * RET_QK + 2 * RET_WIDTH
    u = cols(off, POOL_WIDTH)

    @pl.when(st == 0)
    def _():
        halo_ref[...] = jnp.zeros_like(halo_ref)

    ext = jnp.concatenate([halo_ref[...], u], axis=0)
    halo_ref[...] = u[tm - POOL_HALO:, :]
    group = lax.broadcasted_iota(jnp.int32, (tm, POOL_WIDTH), 1) // POOL_GROUP_DIM
    group_row = lax.broadcasted_iota(jnp.int32, (1, POOL_WIDTH), 1) // POOL_GROUP_DIM
    wsum = None
    window = None
    acc = ext
    for gi, win in enumerate(POOL_WINDOWS):
        acc = acc + pltpu.roll(acc, win // 2, 0)
        cur = acc[POOL_HALO:, :]
        wsum = cur if wsum is None else jnp.where(group == gi, cur, wsum)
        wl = jnp.full((1, POOL_WIDTH), float(win), F32)
        window = wl if window is None else jnp.where(group_row == gi, wl, window)

    def emit_pool(pooled):
        y = _dot(pooled.astype(BF16), poolw_ref[...]) * poolscale_ref[...]
        ypool_ref[0] = y.astype(BF16)

    @pl.when(st == 0)
    def _():
        pos = (lax.broadcasted_iota(jnp.int32, (tm, POOL_WIDTH), 0) + 1).astype(F32)
        emit_pool(wsum / jnp.minimum(pos, window) - u)

    @pl.when(st > 0)
    def _():
        emit_pool(wsum * (1.0 / window) - u)

    off += POOL_WIDTH
    mq = cols(off, MOBA_WIDTH) * (MOBA_HEAD_DIM ** -0.5 * LOG2_E)
    mqt_ref[0] = mq.T.astype(BF16)
    mk_ref[0] = cols(off + MOBA_WIDTH, MOBA_WIDTH).astype(BF16)
    mv = cols(off + 2 * MOBA_WIDTH, MOBA_WIDTH)
    ones = jnp.ones((MOBA_VT_ROWS - MOBA_HEAD_DIM, MOBA_BLOCK), F32)
    for j in range(tm // MOBA_BLOCK):
        mvt = mv[j * MOBA_BLOCK:(j + 1) * MOBA_BLOCK, :].T
        pieces = []
        for hd in range(MOBA_HEADS):
            pieces += [mvt[hd * MOBA_HEAD_DIM:(hd + 1) * MOBA_HEAD_DIM, :], ones]
        mvt_ref[0, j] = jnp.concatenate(pieces, axis=0).astype(BF16)


def _proj(x, gain, w_in, cos, sin, pool_w, pool_scale, layer):
    b, s, _ = x.shape
    tm = TOKEN_TILE
    in_cols = w_in.shape[-1]
    nb = s // MOBA_BLOCK
    bpt = tm // MOBA_BLOCK
    tok = lambda width: pl.BlockSpec((1, tm, width), lambda bi, si: (bi, si, 0))
    out_shape = [
        jax.ShapeDtypeStruct((b, s, RET_QK), BF16),
        jax.ShapeDtypeStruct((b, s, RET_QK), BF16),
        jax.ShapeDtypeStruct((b, s, RET_WIDTH), BF16),
        jax.ShapeDtypeStruct((b, s, RET_WIDTH), BF16),
        jax.ShapeDtypeStruct((b, s, POOL_WIDTH), BF16),
        jax.ShapeDtypeStruct((b, MOBA_WIDTH, s), BF16),
        jax.ShapeDtypeStruct((b, s, MOBA_WIDTH), BF16),
        jax.ShapeDtypeStruct((b, nb, MOBA_HEADS * MOBA_VT_ROWS, MOBA_BLOCK), BF16),
    ]
    out_specs = [
        tok(RET_QK), tok(RET_QK), tok(RET_WIDTH), tok(RET_WIDTH), tok(POOL_WIDTH),
        pl.BlockSpec((1, MOBA_WIDTH, tm), lambda bi, si: (bi, 0, si)),
        tok(MOBA_WIDTH),
        pl.BlockSpec((1, bpt, MOBA_HEADS * MOBA_VT_ROWS, MOBA_BLOCK), lambda bi, si: (bi, si, 0, 0)),
    ]
    return pl.pallas_call(
        _proj_kernel,
        grid=(b, s // tm),
        in_specs=[
            pl.BlockSpec((1, tm, D_MODEL), lambda bi, si: (bi, si, 0)),
            _resident((None, 1, D_MODEL), lambda bi, si: (layer, 0, 0)),
            _resident((None, D_MODEL, in_cols), lambda bi, si: (layer, 0, 0)),
            pl.BlockSpec((tm, RET_QK), lambda bi, si: (si, 0)),
            pl.BlockSpec((tm, RET_QK), lambda bi, si: (si, 0)),
            _resident((None, POOL_WIDTH, POOL_WIDTH), lambda bi, si: (layer, 0, 0)),
            _resident((None, 1, POOL_WIDTH), lambda bi, si: (layer, 0, 0)),
        ],
        out_specs=out_specs,
        out_shape=out_shape,
        scratch_shapes=[pltpu.VMEM((POOL_HALO, POOL_WIDTH), F32)],
        compiler_params=_params("parallel", "arbitrary"),
        name="proj",
    )(x, gain, w_in, cos, sin, pool_w, pool_scale)


def _retention_kernel(q_ref, k_ref, v_ref, sg_ref, dmask_ref, xi_ref, zeta_ref, decay_ref,
                      ondiag_ref, o_ref, state_ref):
    @pl.when(pl.program_id(1) == 0)
    def _():
        state_ref[...] = jnp.zeros_like(state_ref)

    q = q_ref[0]
    k = k_ref[0]
    v = v_ref[0]
    chunk = q.shape[0]
    state = state_ref[...]
    cross = _dot(q, state.astype(BF16)) * xi_ref[...]
    lane_head = lax.broadcasted_iota(jnp.int32, (chunk, RET_QK), 1) // RET_QK_DIM
    for hd in range(RET_HEADS):
        vcols = slice(hd * RET_V_DIM, (hd + 1) * RET_V_DIM)
        qh = jnp.where(lane_head == hd, q, jnp.zeros_like(q))
        inner = (_dot_nt(qh, k) * dmask_ref[hd]).astype(BF16)
        o = _dot(inner, v[:, vcols]) + cross[:, vcols]
        mu = jnp.mean(o, axis=-1, keepdims=True)
        var = jnp.mean(jnp.square(o - mu), axis=-1, keepdims=True)
        o = (o - mu) * lax.rsqrt(var + EPS)
        o_ref[0, :, vcols] = (o * sg_ref[0, :, vcols].astype(F32)).astype(BF16)

    kz = (k.astype(F32) * zeta_ref[...]).astype(BF16)
    state_ref[...] = decay_ref[...] * state + ondiag_ref[...] * _dot_tn(kz, v)


def _retention(rq, rk, rv, sg, dmask, xi, zeta, decay, on_diag):
    b, s, _ = rq.shape
    c = RET_CHUNK
    tok = lambda width: pl.BlockSpec((1, c, width), lambda bi, ci: (bi, ci, 0))
    const = lambda a: _resident(a.shape, lambda bi, ci: (0,) * a.ndim)
    return pl.pallas_call(
        _retention_kernel,
        grid=(b, s // c),
        in_specs=[tok(RET_QK), tok(RET_QK), tok(RET_WIDTH), tok(RET_WIDTH),
                  const(dmask), const(xi), const(zeta), const(decay), const(on_diag)],
        out_specs=tok(RET_WIDTH),
        out_shape=jax.ShapeDtypeStruct((b, s, RET_WIDTH), BF16),
        scratch_shapes=[pltpu.VMEM((RET_QK, RET_WIDTH), F32)],
        compiler_params=_params("parallel", "arbitrary"),
        name="retention",
    )(rq, rk, rv, sg, dmask, xi, zeta, decay, on_diag)


def _retention_tables():
    c = RET_CHUNK
    log_gamma = jnp.log(1.0 - jnp.power(2.0, -5.0 - jnp.arange(RET_HEADS, dtype=F32)))
    idx = jnp.arange(c, dtype=F32)
    diff = idx[:, None] - idx[None, :]
    dmask = jnp.where(diff >= 0, jnp.exp(log_gamma[:, None, None] * jnp.maximum(diff, 0.0)), 0.0)
    xi = jnp.exp(log_gamma[:, None] * (idx + 1.0))
    zeta = jnp.exp(log_gamma[:, None] * (c - 1.0 - idx))
    g_chunk = jnp.exp(log_gamma * c)
    xi_t = jnp.repeat(xi.T, RET_V_DIM, axis=1)
    zeta_t = jnp.repeat(zeta.T, RET_QK_DIM, axis=1)
    head_of_row = np.arange(RET_QK) // RET_QK_DIM
    head_of_col = np.arange(RET_WIDTH) // RET_V_DIM
    on_diag = jnp.asarray(head_of_row[:, None] == head_of_col[None, :], dtype=F32)
    decay = on_diag * jnp.repeat(g_chunk, RET_QK_DIM)[:, None]
    return dmask, xi_t, zeta_t, decay, on_diag


def _rotation_tables(seq):
    pos = jnp.arange(seq, dtype=F32)
    angle = 1.0 / (10000.0 ** jnp.linspace(0.0, 1.0, RET_QK_DIM // 2, dtype=F32))
    angle = jnp.repeat(angle, 2)
    ang = pos[:, None] * angle[None, :]
    return jnp.tile(jnp.cos(ang), (1, RET_HEADS)), jnp.tile(jnp.sin(ang), (1, RET_HEADS))


def _moba_kernel(qt_ref, k_ref, vt_ref, o_ref,
                 kmean_ref, qh_ref, pen_ref, m_ref, acc_ref, p_ref,
                 s_even_ref, s_odd_ref, top_even_ref, top_odd_ref):
    qi = pl.program_id(1)
    nb = kmean_ref.shape[0]
    blk = MOBA_BLOCK
    tq = qt_ref.shape[2]
    heads = range(MOBA_HEADS)
    vt_rows = [slice(hd * MOBA_VT_ROWS, (hd + 1) * MOBA_VT_ROWS) for hd in heads]
    chunks = [slice(r, r + MOBA_ROW_CHUNK) for r in range(0, blk, MOBA_ROW_CHUNK)]

    @pl.when(qi == 0)
    def _():
        for j in range(nb):
            kb = k_ref[0, j * blk:(j + 1) * blk, :].astype(F32)
            kmean_ref[j:j + 1, :] = jnp.mean(kb, axis=0, keepdims=True)

    qt = qt_ref[0]
    row_head = lax.broadcasted_iota(jnp.int32, (MOBA_WIDTH, tq), 0) // MOBA_HEAD_DIM
    for hd in heads:
        qh_ref[hd] = jnp.where(row_head == hd, qt, jnp.zeros_like(qt))

    kmean = kmean_ref[...].astype(BF16)
    blk_id = lax.broadcasted_iota(jnp.int32, (nb, tq), 0)
    past = blk_id < qi
    for hd in heads:
        gate = jnp.where(past, _dot(kmean, qh_ref[hd]), NEG_INF)
        chosen = jnp.zeros((nb, tq), jnp.bool_)
        for _ in range(MOBA_TOPK):
            best = jnp.max(gate, axis=0, keepdims=True)
            idx = jnp.min(jnp.where(gate == best, blk_id, nb), axis=0, keepdims=True)
            hit = blk_id == idx
            chosen = jnp.logical_or(chosen, jnp.logical_and(hit, past))
            gate = jnp.where(hit, NEG_INF, gate)
        pen_ref[hd] = jnp.where(chosen, 0.0, float("inf"))

    m_ref[...] = jnp.full(m_ref.shape, MOBA_M_INIT, F32)
    acc_ref[...] = jnp.zeros_like(acc_ref)

    def col_max(tiles):
        top = tiles[0]
        for t in tiles[1:]:
            top = jnp.maximum(top, t)
        return jnp.max(top, axis=0, keepdims=True)

    def scores(j, buf):
        s_ref, top_ref = buf
        kj = k_ref[0, pl.ds(pl.multiple_of(j * blk, blk), blk), :]
        for hd in heads:
            s = _dot(kj, qh_ref[hd])
            s_ref[hd] = s
            top_ref[hd:hd + 1, :] = col_max([s[c] for c in chunks])

    def attend(j, buf, own):
        s_ref, top_ref = buf
        for hd in heads:
            def chunk(c):
                s = s_ref[hd, c, :]
                if own:
                    key_pos = c.start + lax.broadcasted_iota(jnp.int32, s.shape, 0)
                    qry_pos = lax.broadcasted_iota(jnp.int32, s.shape, 1)
                    s = jnp.where(key_pos <= qry_pos, s, NEG_INF)
                return s

            m_old = m_ref[hd:hd + 1, :]
            if own:
                m_new = jnp.maximum(m_old, col_max([chunk(c) for c in chunks]))
                shift = m_new
            else:
                pen = pen_ref[hd, pl.ds(j, 1), :]
                m_new = jnp.maximum(m_old, top_ref[hd:hd + 1, :] - pen)
                shift = m_new + pen
            m_ref[hd:hd + 1, :] = m_new
            for c in chunks:
                p_ref[hd, c, :] = jnp.exp2((chunk(c) - shift).astype(BF16))
            pv = _dot(vt_ref[0, j, vt_rows[hd], :], p_ref[hd])
            acc_ref[vt_rows[hd], :] = jnp.exp2(m_old - m_new) * acc_ref[vt_rows[hd], :] + pv

    even = (s_even_ref, top_even_ref)
    odd = (s_odd_ref, top_odd_ref)
    scores(0, even)

    def body(t, carry):
        j = 2 * t
        scores(j + 1, odd)
        attend(j, even, own=False)
        scores(jnp.minimum(j + 2, qi), even)
        attend(j + 1, odd, own=False)
        return carry

    lax.fori_loop(0, (qi + 1) // 2, body, 0)
    attend(qi, even, own=True)

    out = []
    for hd in heads:
        base = hd * MOBA_VT_ROWS
        denom = acc_ref[base + MOBA_HEAD_DIM:base + MOBA_HEAD_DIM + 1, :]
        out.append(acc_ref[base:base + MOBA_HEAD_DIM, :] / denom)
    o_ref[0] = jnp.concatenate(out, axis=0).T.astype(BF16)


def _moba(mqt, mk, mvt):
    b, s, _ = mk.shape
    nb = s // MOBA_BLOCK
    tq = MOBA_BLOCK
    vt_rows = MOBA_HEADS * MOBA_VT_ROWS
    scores_buf = pltpu.VMEM((MOBA_HEADS, MOBA_BLOCK, tq), F32)
    head_rows_buf = pltpu.VMEM((8, tq), F32)
    return pl.pallas_call(
        _moba_kernel,
        grid=(b, nb),
        in_specs=[
            pl.BlockSpec((1, MOBA_WIDTH, tq), lambda bi, qi: (bi, 0, qi)),
            pl.BlockSpec((1, s, MOBA_WIDTH), lambda bi, qi: (bi, 0, 0)),
            pl.BlockSpec((1, nb, vt_rows, MOBA_BLOCK), lambda bi, qi: (bi, 0, 0, 0)),
        ],
        out_specs=pl.BlockSpec((1, tq, MOBA_WIDTH), lambda bi, qi: (bi, qi, 0)),
        out_shape=jax.ShapeDtypeStruct((b, s, MOBA_WIDTH), BF16),
        scratch_shapes=[
            pltpu.VMEM((nb, MOBA_WIDTH), F32),
            pltpu.VMEM((MOBA_HEADS, MOBA_WIDTH, tq), BF16),
            pltpu.VMEM((MOBA_HEADS, nb, tq), F32),
            head_rows_buf,
            pltpu.VMEM((vt_rows, tq), F32),
            pltpu.VMEM((MOBA_HEADS, MOBA_BLOCK, tq), BF16),
            scores_buf, scores_buf,
            head_rows_buf, head_rows_buf,
        ],
        compiler_params=_params("parallel", "arbitrary"),
        name="moba",
    )(mqt, mk, mvt)


def _tail_kernel(x_ref, yr_ref, yp_ref, ym_ref, p_ref, wout_ref,
                 ffn_gain_ref, wg_ref, wu_ref, wd_ref,
                 ple_gain_ref, ple_wgate_ref, ple_wproj_ref, final_gain_ref,
                 o_ref, act_ref, *, final):
    x = x_ref[...]
    x = x + _dot(yr_ref[...], wout_ref[:RET_WIDTH, :])
    x = x + _dot(yp_ref[...], wout_ref[RET_WIDTH:RET_WIDTH + POOL_WIDTH, :])
    x = x + _dot(ym_ref[...], wout_ref[RET_WIDTH + POOL_WIDTH:, :])
    x = _swiglu_half_step(x, ffn_gain_ref[...], wg_ref, wu_ref, wd_ref, act_ref)
    h = _rms(x, ple_gain_ref[...]).astype(BF16)
    gate = jax.nn.sigmoid(_dot(h, ple_wgate_ref[...]))
    x = x + gate * _dot(p_ref[...].astype(BF16), ple_wproj_ref[...])
    if final:
        x = _rms(x, final_gain_ref[...])
    o_ref[...] = x


def _tail(x, y_ret, y_pool, y_moba, p, w_out, ffn, ple_gain, ple_w_gate, ple_w_proj,
          final_gain, layer, final):
    tokens = x.shape[0]
    tm = TOKEN_TILE
    ffn_gain, wg, wu, wd = ffn
    tok = lambda width: pl.BlockSpec((tm, width), lambda i: (i, 0))
    per_layer = lambda *shape: _resident((None,) + shape, lambda i: (layer,) + (0,) * len(shape))
    return pl.pallas_call(
        functools.partial(_tail_kernel, final=final),
        grid=(tokens // tm,),
        in_specs=[
            tok(D_MODEL), tok(RET_WIDTH), tok(POOL_WIDTH), tok(MOBA_WIDTH),
            pl.BlockSpec((None, tm, PLE_DIM), lambda i: (layer, i, 0)),
            per_layer(D_MODEL, D_MODEL),
            per_layer(1, D_MODEL), per_layer(D_MODEL, D_FF), per_layer(D_MODEL, D_FF),
            per_layer(D_FF, D_MODEL),
            per_layer(1, D_MODEL), per_layer(D_MODEL, D_MODEL), per_layer(PLE_DIM, D_MODEL),
            _resident((1, D_MODEL), lambda i: (0, 0)),
        ],
        out_specs=tok(D_MODEL),
        out_shape=jax.ShapeDtypeStruct(x.shape, F32),
        scratch_shapes=[pltpu.VMEM((tm, D_FF), BF16)],
        compiler_params=_params("parallel"),
        name="tail",
    )(x, y_ret, y_pool, y_moba, p, w_out, ffn_gain, wg, wu, wd,
      ple_gain, ple_w_gate, ple_w_proj, final_gain)


def _block_diag(w):
    depth, groups, d, _ = w.shape
    eye = jnp.eye(groups, dtype=w.dtype)
    return jnp.einsum("lgcd,gh->lgchd", w, eye).reshape(depth, groups * d, groups * d)


def kernel(x, p, norm_ffn1, ffn1_w_gate, ffn1_w_up, ffn1_w_down, norm_mix, w_in, pool_w, pool_scale, w_out, norm_ffn2, ffn2_w_gate, ffn2_w_up, ffn2_w_down, norm_ple, ple_w_gate, ple_w_proj, norm_final):
    b, s, d = x.shape
    depth = w_in.shape[0]
    tokens = b * s
    bf = lambda w: w.astype(BF16)
    row = lambda g: g.reshape(g.shape[0], 1, g.shape[1])

    ffn1 = (row(norm_ffn1), bf(ffn1_w_gate), bf(ffn1_w_up), bf(ffn1_w_down))
    ffn2 = (row(norm_ffn2), bf(ffn2_w_gate), bf(ffn2_w_up), bf(ffn2_w_down))
    w_in_b, w_out_b = bf(w_in), bf(w_out)
    pool_w_b = bf(_block_diag(pool_w))
    ple_gate_b, ple_proj_b = bf(ple_w_gate), bf(ple_w_proj)
    norm_mix_r, norm_ple_r, pool_scale_r = row(norm_mix), row(norm_ple), row(pool_scale)
    final_gain = norm_final.reshape(1, d)
    p_flat = p.reshape(depth, tokens, PLE_DIM)
    cos, sin = _rotation_tables(s)
    ret_tables = _retention_tables()

    xf = x.reshape(tokens, d)
    for i in range(depth):
        xf = _ffn(xf, *ffn1, i)
        rq, rk, rv, sg, y_pool, mqt, mk, mvt = _proj(
            xf.reshape(b, s, d), norm_mix_r, w_in_b, cos, sin, pool_w_b, pool_scale_r, i)
        y_ret = _retention(rq, rk, rv, sg, *ret_tables)
        y_moba = _moba(mqt, mk, mvt)
        xf = _tail(xf, y_ret.reshape(tokens, RET_WIDTH), y_pool.reshape(tokens, POOL_WIDTH),
                   y_moba.reshape(tokens, MOBA_WIDTH), p_flat, w_out_b, ffn2,
                   norm_ple_r, ple_gate_b, ple_proj_b, final_gain, i, i == depth - 1)
    return xf.reshape(b, s, d)
```

```python
import functools

import jax
import jax.numpy as jnp
import numpy as np
from jax import lax
from jax.experimental import pallas as pl
from jax.experimental.pallas import tpu as pltpu

D_MODEL = 1024
D_FF = 2816
RET_HEADS = 4
RET_QK_DIM = 64
RET_V_DIM = 128
RET_CHUNK = 256
RET_STEP_CHUNKS = 4
RET_QK = RET_HEADS * RET_QK_DIM
RET_WIDTH = RET_HEADS * RET_V_DIM
POOL_GROUPS = 4
POOL_WINDOWS = (2, 4, 8, 16)
POOL_GROUP_DIM = 64
POOL_WIDTH = POOL_GROUPS * POOL_GROUP_DIM
POOL_HALO = 16
assert POOL_WINDOWS == tuple(2 ** (i + 1) for i in range(POOL_GROUPS)) and POOL_HALO == POOL_WINDOWS[-1]
MOBA_HEADS = 4
MOBA_HEAD_DIM = 64
MOBA_WIDTH = MOBA_HEADS * MOBA_HEAD_DIM
MOBA_BLOCK = 256
MOBA_TOPK = 3
MOBA_VT_ROWS = MOBA_HEAD_DIM + 16
MOBA_ROW_CHUNK = 64
MOBA_M_INIT = -(2.0 ** 100)
PLE_DIM = 256
EPS = 1e-6

FF_CHUNK = 256
TOKEN_TILE = 512
VMEM_LIMIT = 56 * 1024 * 1024

BF16 = jnp.bfloat16
F32 = jnp.float32
NEG_INF = float("-inf")
LOG2_E = 1.4426950408889634


def _params(*semantics):
    return pltpu.CompilerParams(dimension_semantics=semantics, vmem_limit_bytes=VMEM_LIMIT)


def _resident(shape, index_map):
    return pl.BlockSpec(shape, index_map, pipeline_mode=pl.Buffered(1))


def _rms(x, gain):
    return x * lax.rsqrt(jnp.mean(x * x, axis=-1, keepdims=True) + EPS) * gain


def _dot(a, b):
    return jnp.dot(a, b, preferred_element_type=F32)


def _dot_nt(a, b):
    return lax.dot_general(a, b, (((1,), (1,)), ((), ())), preferred_element_type=F32)


def _dot_tn(a, b):
    return lax.dot_general(a, b, (((0,), (0,)), ((), ())), preferred_element_type=F32)


def _swiglu_half_step(x, gain, wg_ref, wu_ref, wd_ref, act_ref):
    h = _rms(x, gain).astype(BF16)
    for c in range(D_FF // FF_CHUNK):
        cols = slice(c * FF_CHUNK, (c + 1) * FF_CHUNK)
        gate = _dot(h, wg_ref[:, cols])
        up = _dot(h, wu_ref[:, cols])
        act_ref[:, cols] = (gate * jax.nn.sigmoid(gate) * up).astype(BF16)
    return x + 0.5 * _dot(act_ref[...], wd_ref[...])


def _ffn_kernel(x_ref, gain_ref, wg_ref, wu_ref, wd_ref, o_ref, act_ref):
    o_ref[...] = _swiglu_half_step(x_ref[...], gain_ref[...], wg_ref, wu_ref, wd_ref, act_ref)


def _ffn(x, gain, wg, wu, wd, layer):
    tokens = x.shape[0]
    tm = TOKEN_TILE
    return pl.pallas_call(
        _ffn_kernel,
        grid=(tokens // tm,),
        in_specs=[
            pl.BlockSpec((tm, D_MODEL), lambda i: (i, 0)),
            _resident((None, 1, D_MODEL), lambda i: (layer, 0, 0)),
            _resident((None, D_MODEL, D_FF), lambda i: (layer, 0, 0)),
            _resident((None, D_MODEL, D_FF), lambda i: (layer, 0, 0)),
            _resident((None, D_FF, D_MODEL), lambda i: (layer, 0, 0)),
        ],
        out_specs=pl.BlockSpec((tm, D_MODEL), lambda i: (i, 0)),
        out_shape=jax.ShapeDtypeStruct(x.shape, F32),
        scratch_shapes=[pltpu.VMEM((tm, D_FF), BF16)],
        compiler_params=_params("parallel"),
        name="ffn",
    )(x, gain, wg, wu, wd)


def _rotate_every_two(t, even_lane):
    n = t.shape[-1]
    nxt = pltpu.roll(t, n - 1, 1)
    prv = pltpu.roll(t, 1, 1)
    return jnp.where(even_lane, -nxt, prv)


def _proj_kernel(x_ref, gain_ref, w_ref, cos_ref, sin_ref, poolw_ref, poolscale_ref,
                 rq_ref, rk_ref, rv_ref, sg_ref, ypool_ref, mqt_ref, mk_ref, mvt_ref,
                 halo_ref):
    tm = x_ref.shape[1]
    st = pl.program_id(1)
    h = _rms(x_ref[0], gain_ref[...]).astype(BF16)

    def cols(lo, width):
        return _dot(h, w_ref[:, lo:lo + width])

    cos = cos_ref[...]
    sin = sin_ref[...]
    even_lane = (lax.broadcasted_iota(jnp.int32, (tm, RET_QK), 1) % 2) == 0
    q = cols(0, RET_QK)
    rq_ref[0] = (q * cos + _rotate_every_two(q, even_lane) * sin).astype(BF16)
    k = cols(RET_QK, RET_QK)
    k = (k * cos + _rotate_every_two(k, even_lane) * sin) * (RET_QK_DIM ** -0.5)
    rk_ref[0] = k.astype(BF16)
    rv_ref[0] = cols(2 * RET_QK, RET_WIDTH).astype(BF16)
    g = cols(2 * RET_QK + RET_WIDTH, RET_WIDTH)
    sg_ref[0] = (g * jax.nn.sigmoid(g)).astype(BF16)

    off = 2 * RET_QK + 2 * RET_WIDTH
    u = cols(off, POOL_WIDTH)

    @pl.when(st == 0)
    def _():
        halo_ref[...] = jnp.zeros_like(halo_ref)

    ext = jnp.concatenate([halo_ref[...], u], axis=0)
    halo_ref[...] = u[tm - POOL_HALO:, :]
    group = lax.broadcasted_iota(jnp.int32, (tm, POOL_WIDTH), 1) // POOL_GROUP_DIM
    group_row = lax.broadcasted_iota(jnp.int32, (1, POOL_WIDTH), 1) // POOL_GROUP_DIM
    wsum = None
    window = None
    acc = ext
    for gi, win in enumerate(POOL_WINDOWS):
        acc = acc + pltpu.roll(acc, win // 2, 0)
        cur = acc[POOL_HALO:, :]
        wsum = cur if wsum is None else jnp.where(group == gi, cur, wsum)
        wl = jnp.full((1, POOL_WIDTH), float(win), F32)
        window = wl if window is None else jnp.where(group_row == gi, wl, window)

    def emit_pool(pooled):
        y = _dot(pooled.astype(BF16), poolw_ref[...]) * poolscale_ref[...]
        ypool_ref[0] = y.astype(BF16)

    @pl.when(st == 0)
    def _():
        pos = (lax.broadcasted_iota(jnp.int32, (tm, POOL_WIDTH), 0) + 1).astype(F32)
        emit_pool(wsum / jnp.minimum(pos, window) - u)

    @pl.when(st > 0)
    def _():
        emit_pool(wsum * (1.0 / window) - u)

    off += POOL_WIDTH
    mq = cols(off, MOBA_WIDTH) * (MOBA_HEAD_DIM ** -0.5 * LOG2_E)
    mqt_ref[0] = mq.T.astype(BF16)
    mk_ref[0] = cols(off + MOBA_WIDTH, MOBA_WIDTH).astype(BF16)
    mv = cols(off + 2 * MOBA_WIDTH, MOBA_WIDTH)
    ones = jnp.ones((MOBA_VT_ROWS - MOBA_HEAD_DIM, MOBA_BLOCK), F32)
    for j in range(tm // MOBA_BLOCK):
        mvt = mv[j * MOBA_BLOCK:(j + 1) * MOBA_BLOCK, :].T
        pieces = []
        for hd in range(MOBA_HEADS):
            pieces += [mvt[hd * MOBA_HEAD_DIM:(hd + 1) * MOBA_HEAD_DIM, :], ones]
        mvt_ref[0, j] = jnp.concatenate(pieces, axis=0).astype(BF16)


def _proj(x, gain, w_in, cos, sin, pool_w, pool_scale, layer):
    b, s, _ = x.shape
    tm = TOKEN_TILE
    in_cols = w_in.shape[-1]
    nb = s // MOBA_BLOCK
    bpt = tm // MOBA_BLOCK
    tok = lambda width: pl.BlockSpec((1, tm, width), lambda bi, si: (bi, si, 0))
    out_shape = [
        jax.ShapeDtypeStruct((b, s, RET_QK), BF16),
        jax.ShapeDtypeStruct((b, s, RET_QK), BF16),
        jax.ShapeDtypeStruct((b, s, RET_WIDTH), BF16),
        jax.ShapeDtypeStruct((b, s, RET_WIDTH), BF16),
        jax.ShapeDtypeStruct((b, s, POOL_WIDTH), BF16),
        jax.ShapeDtypeStruct((b, MOBA_WIDTH, s), BF16),
        jax.ShapeDtypeStruct((b, s, MOBA_WIDTH), BF16),
        jax.ShapeDtypeStruct((b, nb, MOBA_HEADS * MOBA_VT_ROWS, MOBA_BLOCK), BF16),
    ]
    out_specs = [
        tok(RET_QK), tok(RET_QK), tok(RET_WIDTH), tok(RET_WIDTH), tok(POOL_WIDTH),
        pl.BlockSpec((1, MOBA_WIDTH, tm), lambda bi, si: (bi, 0, si)),
        tok(MOBA_WIDTH),
        pl.BlockSpec((1, bpt, MOBA_HEADS * MOBA_VT_ROWS, MOBA_BLOCK), lambda bi, si: (bi, si, 0, 0)),
    ]
    return pl.pallas_call(
        _proj_kernel,
        grid=(b, s // tm),
        in_specs=[
            pl.BlockSpec((1, tm, D_MODEL), lambda bi, si: (bi, si, 0)),
            _resident((None, 1, D_MODEL), lambda bi, si: (layer, 0, 0)),
            _resident((None, D_MODEL, in_cols), lambda bi, si: (layer, 0, 0)),
            pl.BlockSpec((tm, RET_QK), lambda bi, si: (si, 0)),
            pl.BlockSpec((tm, RET_QK), lambda bi, si: (si, 0)),
            _resident((None, POOL_WIDTH, POOL_WIDTH), lambda bi, si: (layer, 0, 0)),
            _resident((None, 1, POOL_WIDTH), lambda bi, si: (layer, 0, 0)),
        ],
        out_specs=out_specs,
        out_shape=out_shape,
        scratch_shapes=[pltpu.VMEM((POOL_HALO, POOL_WIDTH), F32)],
        compiler_params=_params("parallel", "arbitrary"),
        name="proj",
    )(x, gain, w_in, cos, sin, pool_w, pool_scale)


def _retention_kernel(q_ref, k_ref, v_ref, sg_ref, dmask_ref, xi_ref, zeta_ref, decay_ref,
                      ondiag_ref, o_ref, state_ref):
    @pl.when(pl.program_id(1) == 0)
    def _():
        state_ref[...] = jnp.zeros_like(state_ref)

    chunk = RET_CHUNK
    lane_head = lax.broadcasted_iota(jnp.int32, (chunk, RET_QK), 1) // RET_QK_DIM
    state = state_ref[...]
    for ci in range(q_ref.shape[1] // chunk):
        rows = slice(ci * chunk, (ci + 1) * chunk)
        q = q_ref[0, rows, :]
        k = k_ref[0, rows, :]
        v = v_ref[0, rows, :]
        cross = _dot(q, state.astype(BF16)) * xi_ref[...]
        for hd in range(RET_HEADS):
            vcols = slice(hd * RET_V_DIM, (hd + 1) * RET_V_DIM)
            qh = jnp.where(lane_head == hd, q, jnp.zeros_like(q))
            inner = (_dot_nt(qh, k) * dmask_ref[hd]).astype(BF16)
            o = _dot(inner, v[:, vcols]) + cross[:, vcols]
            mu = jnp.mean(o, axis=-1, keepdims=True)
            var = jnp.mean(jnp.square(o - mu), axis=-1, keepdims=True)
            o = (o - mu) * lax.rsqrt(var + EPS)
            o_ref[0, rows, vcols] = (o * sg_ref[0, rows, vcols].astype(F32)).astype(BF16)
        kz = (k.astype(F32) * zeta_ref[...]).astype(BF16)
        state = decay_ref[...] * state + ondiag_ref[...] * _dot_tn(kz, v)
    state_ref[...] = state


def _retention(rq, rk, rv, sg, dmask, xi, zeta, decay, on_diag):
    b, s, _ = rq.shape
    c = RET_CHUNK * RET_STEP_CHUNKS
    tok = lambda width: pl.BlockSpec((1, c, width), lambda bi, ci: (bi, ci, 0))
    const = lambda a: _resident(a.shape, lambda bi, ci: (0,) * a.ndim)
    return pl.pallas_call(
        _retention_kernel,
        grid=(b, s // c),
        in_specs=[tok(RET_QK), tok(RET_QK), tok(RET_WIDTH), tok(RET_WIDTH),
                  const(dmask), const(xi), const(zeta), const(decay), const(on_diag)],
        out_specs=tok(RET_WIDTH),
        out_shape=jax.ShapeDtypeStruct((b, s, RET_WIDTH), BF16),
        scratch_shapes=[pltpu.VMEM((RET_QK, RET_WIDTH), F32)],
        compiler_params=_params("parallel", "arbitrary"),
        name="retention",
    )(rq, rk, rv, sg, dmask, xi, zeta, decay, on_diag)


def _retention_tables():
    c = RET_CHUNK
    log_gamma = jnp.log(1.0 - jnp.power(2.0, -5.0 - jnp.arange(RET_HEADS, dtype=F32)))
    idx = jnp.arange(c, dtype=F32)
    diff = idx[:, None] - idx[None, :]
    dmask = jnp.where(diff >= 0, jnp.exp(log_gamma[:, None, None] * jnp.maximum(diff, 0.0)), 0.0)
    xi = jnp.exp(log_gamma[:, None] * (idx + 1.0))
    zeta = jnp.exp(log_gamma[:, None] * (c - 1.0 - idx))
    g_chunk = jnp.exp(log_gamma * c)
    xi_t = jnp.repeat(xi.T, RET_V_DIM, axis=1)
    zeta_t = jnp.repeat(zeta.T, RET_QK_DIM, axis=1)
    head_of_row = np.arange(RET_QK) // RET_QK_DIM
    head_of_col = np.arange(RET_WIDTH) // RET_V_DIM
    on_diag = jnp.asarray(head_of_row[:, None] == head_of_col[None, :], dtype=F32)
    decay = on_diag * jnp.repeat(g_chunk, RET_QK_DIM)[:, None]
    return dmask, xi_t, zeta_t, decay, on_diag


def _rotation_tables(seq):
    pos = jnp.arange(seq, dtype=F32)
    angle = 1.0 / (10000.0 ** jnp.linspace(0.0, 1.0, RET_QK_DIM // 2, dtype=F32))
    angle = jnp.repeat(angle, 2)
    ang = pos[:, None] * angle[None, :]
    return jnp.tile(jnp.cos(ang), (1, RET_HEADS)), jnp.tile(jnp.sin(ang), (1, RET_HEADS))


def _moba_kernel(qt_ref, k_ref, vt_ref, o_ref,
                 kmean_ref, qh_ref, pen_ref, m_ref, acc_ref, p_ref,
                 s_even_ref, s_odd_ref, top_even_ref, top_odd_ref):
    qi = pl.program_id(1)
    nb = kmean_ref.shape[0]
    blk = MOBA_BLOCK
    tq = qt_ref.shape[2]
    heads = range(MOBA_HEADS)
    vt_rows = [slice(hd * MOBA_VT_ROWS, (hd + 1) * MOBA_VT_ROWS) for hd in heads]
    chunks = [slice(r, r + MOBA_ROW_CHUNK) for r in range(0, blk, MOBA_ROW_CHUNK)]

    @pl.when(qi == 0)
    def _():
        for j in range(nb):
            kb = k_ref[0, j * blk:(j + 1) * blk, :].astype(F32)
            kmean_ref[j:j + 1, :] = jnp.mean(kb, axis=0, keepdims=True)

    qt = qt_ref[0]
    row_head = lax.broadcasted_iota(jnp.int32, (MOBA_WIDTH, tq), 0) // MOBA_HEAD_DIM
    for hd in heads:
        qh_ref[hd] = jnp.where(row_head == hd, qt, jnp.zeros_like(qt))

    kmean = kmean_ref[...].astype(BF16)
    blk_id = lax.broadcasted_iota(jnp.int32, (nb, tq), 0)
    past = blk_id < qi
    for hd in heads:
        gate = jnp.where(past, _dot(kmean, qh_ref[hd]), NEG_INF)
        chosen = jnp.zeros((nb, tq), jnp.bool_)
        for _ in range(MOBA_TOPK):
            best = jnp.max(gate, axis=0, keepdims=True)
            idx = jnp.min(jnp.where(gate == best, blk_id, nb), axis=0, keepdims=True)
            hit = blk_id == idx
            chosen = jnp.logical_or(chosen, jnp.logical_and(hit, past))
            gate = jnp.where(hit, NEG_INF, gate)
        pen_ref[hd] = jnp.where(chosen, 0.0, float("inf"))

    m_ref[...] = jnp.full(m_ref.shape, MOBA_M_INIT, F32)
    acc_ref[...] = jnp.zeros_like(acc_ref)

    def col_max(tiles):
        top = tiles[0]
        for t in tiles[1:]:
            top = jnp.maximum(top, t)
        return jnp.max(top, axis=0, keepdims=True)

    def scores(j, buf):
        s_ref, top_ref = buf
        kj = k_ref[0, pl.ds(pl.multiple_of(j * blk, blk), blk), :]
        for hd in heads:
            s = _dot(kj, qh_ref[hd])
            s_ref[hd] = s
            top_ref[hd:hd + 1, :] = col_max([s[c] for c in chunks])

    def attend(j, buf, own):
        s_ref, top_ref = buf
        for hd in heads:
            def chunk(c):
                s = s_ref[hd, c, :]
                if own:
                    key_pos = c.start + lax.broadcasted_iota(jnp.int32, s.shape, 0)
                    qry_pos = lax.broadcasted_iota(jnp.int32, s.shape, 1)
                    s = jnp.where(key_pos <= qry_pos, s, NEG_INF)
                return s

            m_old = m_ref[hd:hd + 1, :]
            if own:
                m_new = jnp.maximum(m_old, col_max([chunk(c) for c in chunks]))
                shift = m_new
            else:
                pen = pen_ref[hd, pl.ds(j, 1), :]
                m_new = jnp.maximum(m_old, top_ref[hd:hd + 1, :] - pen)
                shift = m_new + pen
            m_ref[hd:hd + 1, :] = m_new
            for c in chunks:
                p_ref[hd, c, :] = jnp.exp2((chunk(c) - shift).astype(BF16))
            pv = _dot(vt_ref[0, j, vt_rows[hd], :], p_ref[hd])
            acc_ref[vt_rows[hd], :] = jnp.exp2(m_old - m_new) * acc_ref[vt_rows[hd], :] + pv

    even = (s_even_ref, top_even_ref)
    odd = (s_odd_ref, top_odd_ref)
    scores(0, even)

    def body(t, carry):
        j = 2 * t
        scores(j + 1, odd)
        attend(j, even, own=False)
        scores(jnp.minimum(j + 2, qi), even)
        attend(j + 1, odd, own=False)
        return carry

    lax.fori_loop(0, (qi + 1) // 2, body, 0)
    attend(qi, even, own=True)

    out = []
    for hd in heads:
        base = hd * MOBA_VT_ROWS
        denom = acc_ref[base + MOBA_HEAD_DIM:base + MOBA_HEAD_DIM + 1, :]
        out.append(acc_ref[base:base + MOBA_HEAD_DIM, :] / denom)
    o_ref[0] = jnp.concatenate(out, axis=0).T.astype(BF16)


def _moba(mqt, mk, mvt):
    b, s, _ = mk.shape
    nb = s // MOBA_BLOCK
    tq = MOBA_BLOCK
    vt_rows = MOBA_HEADS * MOBA_VT_ROWS
    scores_buf = pltpu.VMEM((MOBA_HEADS, MOBA_BLOCK, tq), F32)
    head_rows_buf = pltpu.VMEM((8, tq), F32)
    return pl.pallas_call(
        _moba_kernel,
        grid=(b, nb),
        in_specs=[
            pl.BlockSpec((1, MOBA_WIDTH, tq), lambda bi, qi: (bi, 0, qi)),
            pl.BlockSpec((1, s, MOBA_WIDTH), lambda bi, qi: (bi, 0, 0)),
            pl.BlockSpec((1, nb, vt_rows, MOBA_BLOCK), lambda bi, qi: (bi, 0, 0, 0)),
        ],
        out_specs=pl.BlockSpec((1, tq, MOBA_WIDTH), lambda bi, qi: (bi, qi, 0)),
        out_shape=jax.ShapeDtypeStruct((b, s, MOBA_WIDTH), BF16),
        scratch_shapes=[
            pltpu.VMEM((nb, MOBA_WIDTH), F32),
            pltpu.VMEM((MOBA_HEADS, MOBA_WIDTH, tq), BF16),
            pltpu.VMEM((MOBA_HEADS, nb, tq), F32),
            head_rows_buf,
            pltpu.VMEM((vt_rows, tq), F32),
            pltpu.VMEM((MOBA_HEADS, MOBA_BLOCK, tq), BF16),
            scores_buf, scores_buf,
            head_rows_buf, head_rows_buf,
        ],
        compiler_params=_params("parallel", "arbitrary"),
        name="moba",
    )(mqt, mk, mvt)


def _tail_kernel(x_ref, yr_ref, yp_ref, ym_ref, p_ref, wout_ref,
                 ffn_gain_ref, wg_ref, wu_ref, wd_ref,
                 ple_gain_ref, ple_wgate_ref, ple_wproj_ref, final_gain_ref,
                 o_ref, act_ref, *, final):
    x = x_ref[...]
    x = x + _dot(yr_ref[...], wout_ref[:RET_WIDTH, :])
    x = x + _dot(yp_ref[...], wout_ref[RET_WIDTH:RET_WIDTH + POOL_WIDTH, :])
    x = x + _dot(ym_ref[...], wout_ref[RET_WIDTH + POOL_WIDTH:, :])
    x = _swiglu_half_step(x, ffn_gain_ref[...], wg_ref, wu_ref, wd_ref, act_ref)
    h = _rms(x, ple_gain_ref[...]).astype(BF16)
    gate = jax.nn.sigmoid(_dot(h, ple_wgate_ref[...]))
    x = x + gate * _dot(p_ref[...].astype(BF16), ple_wproj_ref[...])
    if final:
        x = _rms(x, final_gain_ref[...])
    o_ref[...] = x


def _tail(x, y_ret, y_pool, y_moba, p, w_out, ffn, ple_gain, ple_w_gate, ple_w_proj,
          final_gain, layer, final):
    tokens = x.shape[0]
    tm = TOKEN_TILE
    ffn_gain, wg, wu, wd = ffn
    tok = lambda width: pl.BlockSpec((tm, width), lambda i: (i, 0))
    per_layer = lambda *shape: _resident((None,) + shape, lambda i: (layer,) + (0,) * len(shape))
    return pl.pallas_call(
        functools.partial(_tail_kernel, final=final),
        grid=(tokens // tm,),
        in_specs=[
            tok(D_MODEL), tok(RET_WIDTH), tok(POOL_WIDTH), tok(MOBA_WIDTH),
            pl.BlockSpec((None, tm, PLE_DIM), lambda i: (layer, i, 0)),
            per_layer(D_MODEL, D_MODEL),
            per_layer(1, D_MODEL), per_layer(D_MODEL, D_FF), per_layer(D_MODEL, D_FF),
            per_layer(D_FF, D_MODEL),
            per_layer(1, D_MODEL), per_layer(D_MODEL, D_MODEL), per_layer(PLE_DIM, D_MODEL),
            _resident((1, D_MODEL), lambda i: (0, 0)),
        ],
        out_specs=tok(D_MODEL),
        out_shape=jax.ShapeDtypeStruct(x.shape, F32),
        scratch_shapes=[pltpu.VMEM((tm, D_FF), BF16)],
        compiler_params=_params("parallel"),
        name="tail",
    )(x, y_ret, y_pool, y_moba, p, w_out, ffn_gain, wg, wu, wd,
      ple_gain, ple_w_gate, ple_w_proj, final_gain)


def _block_diag(w):
    depth, groups, d, _ = w.shape
    eye = jnp.eye(groups, dtype=w.dtype)
    return jnp.einsum("lgcd,gh->lgchd", w, eye).reshape(depth, groups * d, groups * d)


def kernel(x, p, norm_ffn1, ffn1_w_gate, ffn1_w_up, ffn1_w_down, norm_mix, w_in, pool_w, pool_scale, w_out, norm_ffn2, ffn2_w_gate, ffn2_w_up, ffn2_w_down, norm_ple, ple_w_gate, ple_w_proj, norm_final):
    b, s, d = x.shape
    depth = w_in.shape[0]
    tokens = b * s
    bf = lambda w: w.astype(BF16)
    row = lambda g: g.reshape(g.shape[0], 1, g.shape[1])

    ffn1 = (row(norm_ffn1), bf(ffn1_w_gate), bf(ffn1_w_up), bf(ffn1_w_down))
    ffn2 = (row(norm_ffn2), bf(ffn2_w_gate), bf(ffn2_w_up), bf(ffn2_w_down))
    w_in_b, w_out_b = bf(w_in), bf(w_out)
    pool_w_b = bf(_block_diag(pool_w))
    ple_gate_b, ple_proj_b = bf(ple_w_gate), bf(ple_w_proj)
    norm_mix_r, norm_ple_r, pool_scale_r = row(norm_mix), row(norm_ple), row(pool_scale)
    final_gain = norm_final.reshape(1, d)
    p_flat = p.reshape(depth, tokens, PLE_DIM)
    cos, sin = _rotation_tables(s)
    ret_tables = _retention_tables()

    xf = x.reshape(tokens, d)
    for i in range(depth):
        xf = _ffn(xf, *ffn1, i)
        rq, rk, rv, sg, y_pool, mqt, mk, mvt = _proj(
            xf.reshape(b, s, d), norm_mix_r, w_in_b, cos, sin, pool_w_b, pool_scale_r, i)
        y_ret = _retention(rq, rk, rv, sg, *ret_tables)
        y_moba = _moba(mqt, mk, mvt)
        xf = _tail(xf, y_ret.reshape(tokens, RET_WIDTH), y_pool.reshape(tokens, POOL_WIDTH),
                   y_moba.reshape(tokens, MOBA_WIDTH), p_flat, w_out_b, ffn2,
                   norm_ple_r, ple_gate_b, ple_proj_b, final_gain, i, i == depth - 1)
    return xf.reshape(b, s, d)
```

```python
import functools

import jax
import jax.numpy as jnp
import numpy as np
from jax import lax
from jax.experimental import pallas as pl
from jax.experimental.pallas import tpu as pltpu

D_MODEL = 1024
D_FF = 2816
RET_HEADS = 4
RET_QK_DIM = 64
RET_V_DIM = 128
RET_CHUNK = 256
RET_STEP_CHUNKS = 8
RET_QK = RET_HEADS * RET_QK_DIM
RET_WIDTH = RET_HEADS * RET_V_DIM
POOL_GROUPS = 4
POOL_WINDOWS = (2, 4, 8, 16)
POOL_GROUP_DIM = 64
POOL_WIDTH = POOL_GROUPS * POOL_GROUP_DIM
POOL_HALO = 16
assert POOL_WINDOWS == tuple(2 ** (i + 1) for i in range(POOL_GROUPS)) and POOL_HALO == POOL_WINDOWS[-1]
MOBA_HEADS = 4
MOBA_HEAD_DIM = 64
MOBA_WIDTH = MOBA_HEADS * MOBA_HEAD_DIM
MOBA_BLOCK = 256
MOBA_TOPK = 3
MOBA_VT_ROWS = MOBA_HEAD_DIM + 16
MOBA_ROW_CHUNK = 64
MOBA_M_INIT = -(2.0 ** 100)
PLE_DIM = 256
EPS = 1e-6

FF_CHUNK = 256
TOKEN_TILE = 512
VMEM_LIMIT = 56 * 1024 * 1024

BF16 = jnp.bfloat16
F32 = jnp.float32
NEG_INF = float("-inf")
LOG2_E = 1.4426950408889634


def _params(*semantics):
    return pltpu.CompilerParams(dimension_semantics=semantics, vmem_limit_bytes=VMEM_LIMIT)


def _resident(shape, index_map):
    return pl.BlockSpec(shape, index_map, pipeline_mode=pl.Buffered(1))


def _rms(x, gain):
    return x * lax.rsqrt(jnp.mean(x * x, axis=-1, keepdims=True) + EPS) * gain


def _dot(a, b):
    return jnp.dot(a, b, preferred_element_type=F32)


def _dot_nt(a, b):
    return lax.dot_general(a, b, (((1,), (1,)), ((), ())), preferred_element_type=F32)


def _dot_tn(a, b):
    return lax.dot_general(a, b, (((0,), (0,)), ((), ())), preferred_element_type=F32)


def _swiglu_half_step(x, gain, wg_ref, wu_ref, wd_ref, act_ref):
    h = _rms(x, gain).astype(BF16)
    for c in range(D_FF // FF_CHUNK):
        cols = slice(c * FF_CHUNK, (c + 1) * FF_CHUNK)
        gate = _dot(h, wg_ref[:, cols])
        up = _dot(h, wu_ref[:, cols])
        act_ref[:, cols] = (gate * jax.nn.sigmoid(gate) * up).astype(BF16)
    return x + 0.5 * _dot(act_ref[...], wd_ref[...])


def _ffn_kernel(x_ref, gain_ref, wg_ref, wu_ref, wd_ref, o_ref, act_ref):
    o_ref[...] = _swiglu_half_step(x_ref[...], gain_ref[...], wg_ref, wu_ref, wd_ref, act_ref)


def _ffn(x, gain, wg, wu, wd, layer):
    tokens = x.shape[0]
    tm = TOKEN_TILE
    return pl.pallas_call(
        _ffn_kernel,
        grid=(tokens // tm,),
        in_specs=[
            pl.BlockSpec((tm, D_MODEL), lambda i: (i, 0)),
            _resident((None, 1, D_MODEL), lambda i: (layer, 0, 0)),
            _resident((None, D_MODEL, D_FF), lambda i: (layer, 0, 0)),
            _resident((None, D_MODEL, D_FF), lambda i: (layer, 0, 0)),
            _resident((None, D_FF, D_MODEL), lambda i: (layer, 0, 0)),
        ],
        out_specs=pl.BlockSpec((tm, D_MODEL), lambda i: (i, 0)),
        out_shape=jax.ShapeDtypeStruct(x.shape, F32),
        scratch_shapes=[pltpu.VMEM((tm, D_FF), BF16)],
        compiler_params=_params("parallel"),
        name="ffn",
    )(x, gain, wg, wu, wd)


def _rotate_every_two(t, even_lane):
    n = t.shape[-1]
    nxt = pltpu.roll(t, n - 1, 1)
    prv = pltpu.roll(t, 1, 1)
    return jnp.where(even_lane, -nxt, prv)


def _proj_kernel(x_ref, gain_ref, w_ref, cos_ref, sin_ref, poolw_ref, poolscale_ref,
                 rq_ref, rk_ref, rv_ref, sg_ref, ypool_ref, mqt_ref, mk_ref, mvt_ref,
                 halo_ref):
    tm = x_ref.shape[1]
    st = pl.program_id(1)
    h = _rms(x_ref[0], gain_ref[...]).astype(BF16)

    def cols(lo, width):
        return _dot(h, w_ref[:, lo:lo + width])

    cos = cos_ref[...]
    sin = sin_ref[...]
    even_lane = (lax.broadcasted_iota(jnp.int32, (tm, RET_QK), 1) % 2) == 0
    q = cols(0, RET_QK)
    rq_ref[0] = (q * cos + _rotate_every_two(q, even_lane) * sin).astype(BF16)
    k = cols(RET_QK, RET_QK)
    k = (k * cos + _rotate_every_two(k, even_lane) * sin) * (RET_QK_DIM ** -0.5)
    rk_ref[0] = k.astype(BF16)
    rv_ref[0] = cols(2 * RET_QK, RET_WIDTH).astype(BF16)
    g = cols(2 * RET_QK + RET_WIDTH, RET_WIDTH)
    sg_ref[0] = (g * jax.nn.sigmoid(g)).astype(BF16)

    off = 2 * RET_QK + 2 * RET_WIDTH
    u = cols(off, POOL_WIDTH)

    @pl.when(st == 0)
    def _():
        halo_ref[...] = jnp.zeros_like(halo_ref)

    ext = jnp.concatenate([halo_ref[...], u], axis=0)
    halo_ref[...] = u[tm - POOL_HALO:, :]
    group = lax.broadcasted_iota(jnp.int32, (tm, POOL_WIDTH), 1) // POOL_GROUP_DIM
    group_row = lax.broadcasted_iota(jnp.int32, (1, POOL_WIDTH), 1) // POOL_GROUP_DIM
    wsum = None
    window = None
    acc = ext
    for gi, win in enumerate(POOL_WINDOWS):
        acc = acc + pltpu.roll(acc, win // 2, 0)
        cur = acc[POOL_HALO:, :]
        wsum = cur if wsum is None else jnp.where(group == gi, cur, wsum)
        wl = jnp.full((1, POOL_WIDTH), float(win), F32)
        window = wl if window is None else jnp.where(group_row == gi, wl, window)

    def emit_pool(pooled):
        y = _dot(pooled.astype(BF16), poolw_ref[...]) * poolscale_ref[...]
        ypool_ref[0] = y.astype(BF16)

    @pl.when(st == 0)
    def _():
        pos = (lax.broadcasted_iota(jnp.int32, (tm, POOL_WIDTH), 0) + 1).astype(F32)
        emit_pool(wsum / jnp.minimum(pos, window) - u)

    @pl.when(st > 0)
    def _():
        emit_pool(wsum * (1.0 / window) - u)

    off += POOL_WIDTH
    mq = cols(off, MOBA_WIDTH) * (MOBA_HEAD_DIM ** -0.5 * LOG2_E)
    mqt_ref[0] = mq.T.astype(BF16)
    mk_ref[0] = cols(off + MOBA_WIDTH, MOBA_WIDTH).astype(BF16)
    mv = cols(off + 2 * MOBA_WIDTH, MOBA_WIDTH)
    ones = jnp.ones((MOBA_VT_ROWS - MOBA_HEAD_DIM, MOBA_BLOCK), F32)
    for j in range(tm // MOBA_BLOCK):
        mvt = mv[j * MOBA_BLOCK:(j + 1) * MOBA_BLOCK, :].T
        pieces = []
        for hd in range(MOBA_HEADS):
            pieces += [mvt[hd * MOBA_HEAD_DIM:(hd + 1) * MOBA_HEAD_DIM, :], ones]
        mvt_ref[0, j] = jnp.concatenate(pieces, axis=0).astype(BF16)


def _proj(x, gain, w_in, cos, sin, pool_w, pool_scale, layer):
    b, s, _ = x.shape
    tm = TOKEN_TILE
    in_cols = w_in.shape[-1]
    nb = s // MOBA_BLOCK
    bpt = tm // MOBA_BLOCK
    tok = lambda width: pl.BlockSpec((1, tm, width), lambda bi, si: (bi, si, 0))
    out_shape = [
        jax.ShapeDtypeStruct((b, s, RET_QK), BF16),
        jax.ShapeDtypeStruct((b, s, RET_QK), BF16),
        jax.ShapeDtypeStruct((b, s, RET_WIDTH), BF16),
        jax.ShapeDtypeStruct((b, s, RET_WIDTH), BF16),
        jax.ShapeDtypeStruct((b, s, POOL_WIDTH), BF16),
        jax.ShapeDtypeStruct((b, MOBA_WIDTH, s), BF16),
        jax.ShapeDtypeStruct((b, s, MOBA_WIDTH), BF16),
        jax.ShapeDtypeStruct((b, nb, MOBA_HEADS * MOBA_VT_ROWS, MOBA_BLOCK), BF16),
    ]
    out_specs = [
        tok(RET_QK), tok(RET_QK), tok(RET_WIDTH), tok(RET_WIDTH), tok(POOL_WIDTH),
        pl.BlockSpec((1, MOBA_WIDTH, tm), lambda bi, si: (bi, 0, si)),
        tok(MOBA_WIDTH),
        pl.BlockSpec((1, bpt, MOBA_HEADS * MOBA_VT_ROWS, MOBA_BLOCK), lambda bi, si: (bi, si, 0, 0)),
    ]
    return pl.pallas_call(
        _proj_kernel,
        grid=(b, s // tm),
        in_specs=[
            pl.BlockSpec((1, tm, D_MODEL), lambda bi, si: (bi, si, 0)),
            _resident((None, 1, D_MODEL), lambda bi, si: (layer, 0, 0)),
            _resident((None, D_MODEL, in_cols), lambda bi, si: (layer, 0, 0)),
            pl.BlockSpec((tm, RET_QK), lambda bi, si: (si, 0)),
            pl.BlockSpec((tm, RET_QK), lambda bi, si: (si, 0)),
            _resident((None, POOL_WIDTH, POOL_WIDTH), lambda bi, si: (layer, 0, 0)),
            _resident((None, 1, POOL_WIDTH), lambda bi, si: (layer, 0, 0)),
        ],
        out_specs=out_specs,
        out_shape=out_shape,
        scratch_shapes=[pltpu.VMEM((POOL_HALO, POOL_WIDTH), F32)],
        compiler_params=_params("parallel", "arbitrary"),
        name="proj",
    )(x, gain, w_in, cos, sin, pool_w, pool_scale)


def _retention_kernel(q_ref, k_ref, v_ref, sg_ref, dmask_ref, xi_ref, zeta_ref, decay_ref,
                      ondiag_ref, o_ref, state_ref):
    @pl.when(pl.program_id(1) == 0)
    def _():
        state_ref[...] = jnp.zeros_like(state_ref)

    chunk = RET_CHUNK
    lane_head = lax.broadcasted_iota(jnp.int32, (chunk, RET_QK), 1) // RET_QK_DIM
    state = state_ref[...]
    for ci in range(q_ref.shape[1] // chunk):
        rows = slice(ci * chunk, (ci + 1) * chunk)
        q = q_ref[0, rows, :]
        k = k_ref[0, rows, :]
        v = v_ref[0, rows, :]
        cross = _dot(q, state.astype(BF16)) * xi_ref[...]
        for hd in range(RET_HEADS):
            vcols = slice(hd * RET_V_DIM, (hd + 1) * RET_V_DIM)
            qh = jnp.where(lane_head == hd, q, jnp.zeros_like(q))
            inner = (_dot_nt(qh, k) * dmask_ref[hd]).astype(BF16)
            o = _dot(inner, v[:, vcols]) + cross[:, vcols]
            mu = jnp.mean(o, axis=-1, keepdims=True)
            var = jnp.mean(jnp.square(o - mu), axis=-1, keepdims=True)
            o = (o - mu) * lax.rsqrt(var + EPS)
            o_ref[0, rows, vcols] = (o * sg_ref[0, rows, vcols].astype(F32)).astype(BF16)
        kz = (k.astype(F32) * zeta_ref[...]).astype(BF16)
        state = decay_ref[...] * state + ondiag_ref[...] * _dot_tn(kz, v)
    state_ref[...] = state


def _retention(rq, rk, rv, sg, dmask, xi, zeta, decay, on_diag):
    b, s, _ = rq.shape
    c = RET_CHUNK * RET_STEP_CHUNKS
    tok = lambda width: pl.BlockSpec((1, c, width), lambda bi, ci: (bi, ci, 0))
    const = lambda a: _resident(a.shape, lambda bi, ci: (0,) * a.ndim)
    return pl.pallas_call(
        _retention_kernel,
        grid=(b, s // c),
        in_specs=[tok(RET_QK), tok(RET_QK), tok(RET_WIDTH), tok(RET_WIDTH),
                  const(dmask), const(xi), const(zeta), const(decay), const(on_diag)],
        out_specs=tok(RET_WIDTH),
        out_shape=jax.ShapeDtypeStruct((b, s, RET_WIDTH), BF16),
        scratch_shapes=[pltpu.VMEM((RET_QK, RET_WIDTH), F32)],
        compiler_params=_params("parallel", "arbitrary"),
        name="retention",
    )(rq, rk, rv, sg, dmask, xi, zeta, decay, on_diag)


def _retention_tables():
    c = RET_CHUNK
    log_gamma = jnp.log(1.0 - jnp.power(2.0, -5.0 - jnp.arange(RET_HEADS, dtype=F32)))
    idx = jnp.arange(c, dtype=F32)
    diff = idx[:, None] - idx[None, :]
    dmask = jnp.where(diff >= 0, jnp.exp(log_gamma[:, None, None] * jnp.maximum(diff, 0.0)), 0.0)
    xi = jnp.exp(log_gamma[:, None] * (idx + 1.0))
    zeta = jnp.exp(log_gamma[:, None] * (c - 1.0 - idx))
    g_chunk = jnp.exp(log_gamma * c)
    xi_t = jnp.repeat(xi.T, RET_V_DIM, axis=1)
    zeta_t = jnp.repeat(zeta.T, RET_QK_DIM, axis=1)
    head_of_row = np.arange(RET_QK) // RET_QK_DIM
    head_of_col = np.arange(RET_WIDTH) // RET_V_DIM
    on_diag = jnp.asarray(head_of_row[:, None] == head_of_col[None, :], dtype=F32)
    decay = on_diag * jnp.repeat(g_chunk, RET_QK_DIM)[:, None]
    return dmask, xi_t, zeta_t, decay, on_diag


def _rotation_tables(seq):
    pos = jnp.arange(seq, dtype=F32)
    angle = 1.0 / (10000.0 ** jnp.linspace(0.0, 1.0, RET_QK_DIM // 2, dtype=F32))
    angle = jnp.repeat(angle, 2)
    ang = pos[:, None] * angle[None, :]
    return jnp.tile(jnp.cos(ang), (1, RET_HEADS)), jnp.tile(jnp.sin(ang), (1, RET_HEADS))


def _moba_kernel(qt_ref, k_ref, vt_ref, o_ref,
                 kmean_ref, qh_ref, pen_ref, m_ref, acc_ref, p_ref,
                 s_even_ref, s_odd_ref, top_even_ref, top_odd_ref):
    qi = pl.program_id(1)
    nb = kmean_ref.shape[0]
    blk = MOBA_BLOCK
    tq = qt_ref.shape[2]
    heads = range(MOBA_HEADS)
    vt_rows = [slice(hd * MOBA_VT_ROWS, (hd + 1) * MOBA_VT_ROWS) for hd in heads]
    chunks = [slice(r, r + MOBA_ROW_CHUNK) for r in range(0, blk, MOBA_ROW_CHUNK)]

    @pl.when(qi == 0)
    def _():
        for j in range(nb):
            kb = k_ref[0, j * blk:(j + 1) * blk, :].astype(F32)
            kmean_ref[j:j + 1, :] = jnp.mean(kb, axis=0, keepdims=True)

    qt = qt_ref[0]
    row_head = lax.broadcasted_iota(jnp.int32, (MOBA_WIDTH, tq), 0) // MOBA_HEAD_DIM
    for hd in heads:
        qh_ref[hd] = jnp.where(row_head == hd, qt, jnp.zeros_like(qt))

    kmean = kmean_ref[...].astype(BF16)
    blk_id = lax.broadcasted_iota(jnp.int32, (nb, tq), 0)
    past = blk_id < qi
    for hd in heads:
        gate = jnp.where(past, _dot(kmean, qh_ref[hd]), NEG_INF)
        chosen = jnp.zeros((nb, tq), jnp.bool_)
        for _ in range(MOBA_TOPK):
            best = jnp.max(gate, axis=0, keepdims=True)
            idx = jnp.min(jnp.where(gate == best, blk_id, nb), axis=0, keepdims=True)
            hit = blk_id == idx
            chosen = jnp.logical_or(chosen, jnp.logical_and(hit, past))
            gate = jnp.where(hit, NEG_INF, gate)
        pen_ref[hd] = jnp.where(chosen, 0.0, float("inf"))

    m_ref[...] = jnp.full(m_ref.shape, MOBA_M_INIT, F32)
    acc_ref[...] = jnp.zeros_like(acc_ref)

    def col_max(tiles):
        top = tiles[0]
        for t in tiles[1:]:
            top = jnp.maximum(top, t)
        return jnp.max(top, axis=0, keepdims=True)

    def scores(j, buf, heads=heads):
        s_ref, top_ref = buf
        kj = k_ref[0, pl.ds(pl.multiple_of(j * blk, blk), blk), :]
        for hd in heads:
            s = _dot(kj, qh_ref[hd])
            s_ref[hd] = s
            top_ref[hd:hd + 1, :] = col_max([s[c] for c in chunks])

    def attend(j, buf, own, heads=heads):
        s_ref, top_ref = buf
        for hd in heads:
            def chunk(c):
                s = s_ref[hd, c, :]
                if own:
                    key_pos = c.start + lax.broadcasted_iota(jnp.int32, s.shape, 0)
                    qry_pos = lax.broadcasted_iota(jnp.int32, s.shape, 1)
                    s = jnp.where(key_pos <= qry_pos, s, NEG_INF)
                return s

            m_old = m_ref[hd:hd + 1, :]
            if own:
                m_new = jnp.maximum(m_old, col_max([chunk(c) for c in chunks]))
                shift = m_new
            else:
                pen = pen_ref[hd, pl.ds(j, 1), :]
                m_new = jnp.maximum(m_old, top_ref[hd:hd + 1, :] - pen)
                shift = m_new + pen
            m_ref[hd:hd + 1, :] = m_new
            for c in chunks:
                p_ref[hd, c, :] = jnp.exp2((chunk(c) - shift).astype(BF16))
            pv = _dot(vt_ref[0, j, vt_rows[hd], :], p_ref[hd])
            acc_ref[vt_rows[hd], :] = jnp.exp2(m_old - m_new) * acc_ref[vt_rows[hd], :] + pv

    HEAD_GROUPS = [(0, 1), (2, 3)]
    even = (s_even_ref, top_even_ref)
    odd = (s_odd_ref, top_odd_ref)
    scores(0, even)

    def body(t, carry):
        j = 2 * t
        for g in HEAD_GROUPS:
            scores(j + 1, odd, g)
            attend(j, even, False, g)
        for g in HEAD_GROUPS:
            scores(jnp.minimum(j + 2, qi), even, g)
            attend(j + 1, odd, False, g)
        return carry

    lax.fori_loop(0, (qi + 1) // 2, body, 0)
    attend(qi, even, own=True)

    out = []
    for hd in heads:
        base = hd * MOBA_VT_ROWS
        denom = acc_ref[base + MOBA_HEAD_DIM:base + MOBA_HEAD_DIM + 1, :]
        out.append(acc_ref[base:base + MOBA_HEAD_DIM, :] / denom)
    o_ref[0] = jnp.concatenate(out, axis=0).T.astype(BF16)


def _moba(mqt, mk, mvt):
    b, s, _ = mk.shape
    nb = s // MOBA_BLOCK
    tq = MOBA_BLOCK
    vt_rows = MOBA_HEADS * MOBA_VT_ROWS
    scores_buf = pltpu.VMEM((MOBA_HEADS, MOBA_BLOCK, tq), F32)
    head_rows_buf = pltpu.VMEM((8, tq), F32)
    return pl.pallas_call(
        _moba_kernel,
        grid=(b, nb),
        in_specs=[
            pl.BlockSpec((1, MOBA_WIDTH, tq), lambda bi, qi: (bi, 0, qi)),
            pl.BlockSpec((1, s, MOBA_WIDTH), lambda bi, qi: (bi, 0, 0)),
            pl.BlockSpec((1, nb, vt_rows, MOBA_BLOCK), lambda bi, qi: (bi, 0, 0, 0)),
        ],
        out_specs=pl.BlockSpec((1, tq, MOBA_WIDTH), lambda bi, qi: (bi, qi, 0)),
        out_shape=jax.ShapeDtypeStruct((b, s, MOBA_WIDTH), BF16),
        scratch_shapes=[
            pltpu.VMEM((nb, MOBA_WIDTH), F32),
            pltpu.VMEM((MOBA_HEADS, MOBA_WIDTH, tq), BF16),
            pltpu.VMEM((MOBA_HEADS, nb, tq), F32),
            head_rows_buf,
            pltpu.VMEM((vt_rows, tq), F32),
            pltpu.VMEM((MOBA_HEADS, MOBA_BLOCK, tq), BF16),
            scores_buf, scores_buf,
            head_rows_buf, head_rows_buf,
        ],
        compiler_params=_params("parallel", "arbitrary"),
        name="moba",
    )(mqt, mk, mvt)


def _tail_kernel(x_ref, yr_ref, yp_ref, ym_ref, p_ref, wout_ref,
                 ffn_gain_ref, wg_ref, wu_ref, wd_ref,
                 ple_gain_ref, ple_wgate_ref, ple_wproj_ref, final_gain_ref,
                 o_ref, act_ref, *, final):
    x = x_ref[...]
    x = x + _dot(yr_ref[...], wout_ref[:RET_WIDTH, :])
    x = x + _dot(yp_ref[...], wout_ref[RET_WIDTH:RET_WIDTH + POOL_WIDTH, :])
    x = x + _dot(ym_ref[...], wout_ref[RET_WIDTH + POOL_WIDTH:, :])
    x = _swiglu_half_step(x, ffn_gain_ref[...], wg_ref, wu_ref, wd_ref, act_ref)
    h = _rms(x, ple_gain_ref[...]).astype(BF16)
    gate = jax.nn.sigmoid(_dot(h, ple_wgate_ref[...]))
    x = x + gate * _dot(p_ref[...].astype(BF16), ple_wproj_ref[...])
    if final:
        x = _rms(x, final_gain_ref[...])
    o_ref[...] = x


def _tail(x, y_ret, y_pool, y_moba, p, w_out, ffn, ple_gain, ple_w_gate, ple_w_proj,
          final_gain, layer, final):
    tokens = x.shape[0]
    tm = TOKEN_TILE
    ffn_gain, wg, wu, wd = ffn
    tok = lambda width: pl.BlockSpec((tm, width), lambda i: (i, 0))
    per_layer = lambda *shape: _resident((None,) + shape, lambda i: (layer,) + (0,) * len(shape))
    return pl.pallas_call(
        functools.partial(_tail_kernel, final=final),
        grid=(tokens // tm,),
        in_specs=[
            tok(D_MODEL), tok(RET_WIDTH), tok(POOL_WIDTH), tok(MOBA_WIDTH),
            pl.BlockSpec((None, tm, PLE_DIM), lambda i: (layer, i, 0)),
            per_layer(D_MODEL, D_MODEL),
            per_layer(1, D_MODEL), per_layer(D_MODEL, D_FF), per_layer(D_MODEL, D_FF),
            per_layer(D_FF, D_MODEL),
            per_layer(1, D_MODEL), per_layer(D_MODEL, D_MODEL), per_layer(PLE_DIM, D_MODEL),
            _resident((1, D_MODEL), lambda i: (0, 0)),
        ],
        out_specs=tok(D_MODEL),
        out_shape=jax.ShapeDtypeStruct(x.shape, F32),
        scratch_shapes=[pltpu.VMEM((tm, D_FF), BF16)],
        compiler_params=_params("parallel"),
        name="tail",
    )(x, y_ret, y_pool, y_moba, p, w_out, ffn_gain, wg, wu, wd,
      ple_gain, ple_w_gate, ple_w_proj, final_gain)


def _block_diag(w):
    depth, groups, d, _ = w.shape
    eye = jnp.eye(groups, dtype=w.dtype)
    return jnp.einsum("lgcd,gh->lgchd", w, eye).reshape(depth, groups * d, groups * d)


def kernel(x, p, norm_ffn1, ffn1_w_gate, ffn1_w_up, ffn1_w_down, norm_mix, w_in, pool_w, pool_scale, w_out, norm_ffn2, ffn2_w_gate, ffn2_w_up, ffn2_w_down, norm_ple, ple_w_gate, ple_w_proj, norm_final):
    b, s, d = x.shape
    depth = w_in.shape[0]
    tokens = b * s
    bf = lambda w: w.astype(BF16)
    row = lambda g: g.reshape(g.shape[0], 1, g.shape[1])

    ffn1 = (row(norm_ffn1), bf(ffn1_w_gate), bf(ffn1_w_up), bf(ffn1_w_down))
    ffn2 = (row(norm_ffn2), bf(ffn2_w_gate), bf(ffn2_w_up), bf(ffn2_w_down))
    w_in_b, w_out_b = bf(w_in), bf(w_out)
    pool_w_b = bf(_block_diag(pool_w))
    ple_gate_b, ple_proj_b = bf(ple_w_gate), bf(ple_w_proj)
    norm_mix_r, norm_ple_r, pool_scale_r = row(norm_mix), row(norm_ple), row(pool_scale)
    final_gain = norm_final.reshape(1, d)
    p_flat = p.reshape(depth, tokens, PLE_DIM)
    cos, sin = _rotation_tables(s)
    ret_tables = _retention_tables()

    xf = x.reshape(tokens, d)
    for i in range(depth):
        xf = _ffn(xf, *ffn1, i)
        rq, rk, rv, sg, y_pool, mqt, mk, mvt = _proj(
            xf.reshape(b, s, d), norm_mix_r, w_in_b, cos, sin, pool_w_b, pool_scale_r, i)
        y_ret = _retention(rq, rk, rv, sg, *ret_tables)
        y_moba = _moba(mqt, mk, mvt)
        xf = _tail(xf, y_ret.reshape(tokens, RET_WIDTH), y_pool.reshape(tokens, POOL_WIDTH),
                   y_moba.reshape(tokens, MOBA_WIDTH), p_flat, w_out_b, ffn2,
                   norm_ple_r, ple_gate_b, ple_proj_b, final_gain, i, i == depth - 1)
    return xf.reshape(b, s, d)
```

```python
import functools

import jax
import jax.numpy as jnp
import numpy as np
from jax import lax
from jax.experimental import pallas as pl
from jax.experimental.pallas import tpu as pltpu

D_MODEL = 1024
D_FF = 2816
RET_HEADS = 4
RET_QK_DIM = 64
RET_V_DIM = 128
RET_CHUNK = 256
RET_STEP_CHUNKS = 8
RET_QK = RET_HEADS * RET_QK_DIM
RET_WIDTH = RET_HEADS * RET_V_DIM
POOL_GROUPS = 4
POOL_WINDOWS = (2, 4, 8, 16)
POOL_GROUP_DIM = 64
POOL_WIDTH = POOL_GROUPS * POOL_GROUP_DIM
POOL_HALO = 16
assert POOL_WINDOWS == tuple(2 ** (i + 1) for i in range(POOL_GROUPS)) and POOL_HALO == POOL_WINDOWS[-1]
MOBA_HEADS = 4
MOBA_HEAD_DIM = 64
MOBA_WIDTH = MOBA_HEADS * MOBA_HEAD_DIM
MOBA_BLOCK = 256
MOBA_TOPK = 3
MOBA_VT_ROWS = MOBA_HEAD_DIM + 16
MOBA_ROW_CHUNK = 64
MOBA_M_INIT = -(2.0 ** 100)
MOBA_BOUND_SLACK = 1.0 + 2.0 ** -7
MOBA_MIN_SUM = 2.0 ** -100
PLE_DIM = 256
EPS = 1e-6

FF_CHUNK = 256
TOKEN_TILE = 512
VMEM_LIMIT = 56 * 1024 * 1024

BF16 = jnp.bfloat16
F32 = jnp.float32
NEG_INF = float("-inf")
LOG2_E = 1.4426950408889634


def _params(*semantics):
    return pltpu.CompilerParams(dimension_semantics=semantics, vmem_limit_bytes=VMEM_LIMIT)


def _resident(shape, index_map):
    return pl.BlockSpec(shape, index_map, pipeline_mode=pl.Buffered(1))


def _rms(x, gain):
    return x * lax.rsqrt(jnp.mean(x * x, axis=-1, keepdims=True) + EPS) * gain


def _dot(a, b):
    return jnp.dot(a, b, preferred_element_type=F32)


def _dot_nt(a, b):
    return lax.dot_general(a, b, (((1,), (1,)), ((), ())), preferred_element_type=F32)


def _dot_tn(a, b):
    return lax.dot_general(a, b, (((0,), (0,)), ((), ())), preferred_element_type=F32)


def _swiglu_half_step(x, gain, wg_ref, wu_ref, wd_ref, act_ref):
    h = _rms(x, gain).astype(BF16)
    for c in range(D_FF // FF_CHUNK):
        cols = slice(c * FF_CHUNK, (c + 1) * FF_CHUNK)
        gate = _dot(h, wg_ref[:, cols])
        up = _dot(h, wu_ref[:, cols])
        act_ref[:, cols] = (gate * jax.nn.sigmoid(gate) * up).astype(BF16)
    return x + 0.5 * _dot(act_ref[...], wd_ref[...])


def _ffn_kernel(x_ref, gain_ref, wg_ref, wu_ref, wd_ref, o_ref, act_ref):
    o_ref[...] = _swiglu_half_step(x_ref[...], gain_ref[...], wg_ref, wu_ref, wd_ref, act_ref)


def _ffn(x, gain, wg, wu, wd, layer):
    tokens = x.shape[0]
    tm = TOKEN_TILE
    return pl.pallas_call(
        _ffn_kernel,
        grid=(tokens // tm,),
        in_specs=[
            pl.BlockSpec((tm, D_MODEL), lambda i: (i, 0)),
            _resident((None, 1, D_MODEL), lambda i: (layer, 0, 0)),
            _resident((None, D_MODEL, D_FF), lambda i: (layer, 0, 0)),
            _resident((None, D_MODEL, D_FF), lambda i: (layer, 0, 0)),
            _resident((None, D_FF, D_MODEL), lambda i: (layer, 0, 0)),
        ],
        out_specs=pl.BlockSpec((tm, D_MODEL), lambda i: (i, 0)),
        out_shape=jax.ShapeDtypeStruct(x.shape, F32),
        scratch_shapes=[pltpu.VMEM((tm, D_FF), BF16)],
        compiler_params=_params("parallel"),
        name="ffn",
    )(x, gain, wg, wu, wd)


def _rotate_every_two(t, even_lane):
    n = t.shape[-1]
    nxt = pltpu.roll(t, n - 1, 1)
    prv = pltpu.roll(t, 1, 1)
    return jnp.where(even_lane, -nxt, prv)


def _proj_kernel(x_ref, gain_ref, w_ref, cos_ref, sin_ref, poolw_ref, poolscale_ref,
                 rq_ref, rk_ref, rv_ref, sg_ref, ypool_ref, mqt_ref, mk_ref, mvt_ref,
                 halo_ref):
    tm = x_ref.shape[1]
    st = pl.program_id(1)
    h = _rms(x_ref[0], gain_ref[...]).astype(BF16)

    def cols(lo, width):
        return _dot(h, w_ref[:, lo:lo + width])

    cos = cos_ref[...]
    sin = sin_ref[...]
    even_lane = (lax.broadcasted_iota(jnp.int32, (tm, RET_QK), 1) % 2) == 0
    q = cols(0, RET_QK)
    rq_ref[0] = (q * cos + _rotate_every_two(q, even_lane) * sin).astype(BF16)
    k = cols(RET_QK, RET_QK)
    k = (k * cos + _rotate_every_two(k, even_lane) * sin) * (RET_QK_DIM ** -0.5)
    rk_ref[0] = k.astype(BF16)
    rv_ref[0] = cols(2 * RET_QK, RET_WIDTH).astype(BF16)
    g = cols(2 * RET_QK + RET_WIDTH, RET_WIDTH)
    sg_ref[0] = (g * jax.nn.sigmoid(g)).astype(BF16)

    off = 2 * RET_QK + 2 * RET_WIDTH
    u = cols(off, POOL_WIDTH)

    @pl.when(st == 0)
    def _():
        halo_ref[...] = jnp.zeros_like(halo_ref)

    ext = jnp.concatenate([halo_ref[...], u], axis=0)
    halo_ref[...] = u[tm - POOL_HALO:, :]
    group = lax.broadcasted_iota(jnp.int32, (tm, POOL_WIDTH), 1) // POOL_GROUP_DIM
    group_row = lax.broadcasted_iota(jnp.int32, (1, POOL_WIDTH), 1) // POOL_GROUP_DIM
    wsum = None
    window = None
    acc = ext
    for gi, win in enumerate(POOL_WINDOWS):
        acc = acc + pltpu.roll(acc, win // 2, 0)
        cur = acc[POOL_HALO:, :]
        wsum = cur if wsum is None else jnp.where(group == gi, cur, wsum)
        wl = jnp.full((1, POOL_WIDTH), float(win), F32)
        window = wl if window is None else jnp.where(group_row == gi, wl, window)

    def emit_pool(pooled):
        y = _dot(pooled.astype(BF16), poolw_ref[...]) * poolscale_ref[...]
        ypool_ref[0] = y.astype(BF16)

    @pl.when(st == 0)
    def _():
        pos = (lax.broadcasted_iota(jnp.int32, (tm, POOL_WIDTH), 0) + 1).astype(F32)
        emit_pool(wsum / jnp.minimum(pos, window) - u)

    @pl.when(st > 0)
    def _():
        emit_pool(wsum * (1.0 / window) - u)

    off += POOL_WIDTH
    mq = cols(off, MOBA_WIDTH) * (MOBA_HEAD_DIM ** -0.5 * LOG2_E)
    mqt_ref[0] = mq.T.astype(BF16)
    mk_ref[0] = cols(off + MOBA_WIDTH, MOBA_WIDTH).astype(BF16)
    mv = cols(off + 2 * MOBA_WIDTH, MOBA_WIDTH)
    ones = jnp.ones((MOBA_VT_ROWS - MOBA_HEAD_DIM, MOBA_BLOCK), F32)
    for j in range(tm // MOBA_BLOCK):
        mvt = mv[j * MOBA_BLOCK:(j + 1) * MOBA_BLOCK, :].T
        pieces = []
        for hd in range(MOBA_HEADS):
            pieces += [mvt[hd * MOBA_HEAD_DIM:(hd + 1) * MOBA_HEAD_DIM, :], ones]
        mvt_ref[0, j] = jnp.concatenate(pieces, axis=0).astype(BF16)


def _proj(x, gain, w_in, cos, sin, pool_w, pool_scale, layer):
    b, s, _ = x.shape
    tm = TOKEN_TILE
    in_cols = w_in.shape[-1]
    nb = s // MOBA_BLOCK
    bpt = tm // MOBA_BLOCK
    tok = lambda width: pl.BlockSpec((1, tm, width), lambda bi, si: (bi, si, 0))
    out_shape = [
        jax.ShapeDtypeStruct((b, s, RET_QK), BF16),
        jax.ShapeDtypeStruct((b, s, RET_QK), BF16),
        jax.ShapeDtypeStruct((b, s, RET_WIDTH), BF16),
        jax.ShapeDtypeStruct((b, s, RET_WIDTH), BF16),
        jax.ShapeDtypeStruct((b, s, POOL_WIDTH), BF16),
        jax.ShapeDtypeStruct((b, MOBA_WIDTH, s), BF16),
        jax.ShapeDtypeStruct((b, s, MOBA_WIDTH), BF16),
        jax.ShapeDtypeStruct((b, nb, MOBA_HEADS * MOBA_VT_ROWS, MOBA_BLOCK), BF16),
    ]
    out_specs = [
        tok(RET_QK), tok(RET_QK), tok(RET_WIDTH), tok(RET_WIDTH), tok(POOL_WIDTH),
        pl.BlockSpec((1, MOBA_WIDTH, tm), lambda bi, si: (bi, 0, si)),
        tok(MOBA_WIDTH),
        pl.BlockSpec((1, bpt, MOBA_HEADS * MOBA_VT_ROWS, MOBA_BLOCK), lambda bi, si: (bi, si, 0, 0)),
    ]
    return pl.pallas_call(
        _proj_kernel,
        grid=(b, s // tm),
        in_specs=[
            pl.BlockSpec((1, tm, D_MODEL), lambda bi, si: (bi, si, 0)),
            _resident((None, 1, D_MODEL), lambda bi, si: (layer, 0, 0)),
            _resident((None, D_MODEL, in_cols), lambda bi, si: (layer, 0, 0)),
            pl.BlockSpec((tm, RET_QK), lambda bi, si: (si, 0)),
            pl.BlockSpec((tm, RET_QK), lambda bi, si: (si, 0)),
            _resident((None, POOL_WIDTH, POOL_WIDTH), lambda bi, si: (layer, 0, 0)),
            _resident((None, 1, POOL_WIDTH), lambda bi, si: (layer, 0, 0)),
        ],
        out_specs=out_specs,
        out_shape=out_shape,
        scratch_shapes=[pltpu.VMEM((POOL_HALO, POOL_WIDTH), F32)],
        compiler_params=_params("parallel", "arbitrary"),
        name="proj",
    )(x, gain, w_in, cos, sin, pool_w, pool_scale)


def _retention_kernel(q_ref, k_ref, v_ref, sg_ref, dmask_ref, xi_ref, zeta_ref, decay_ref,
                      ondiag_ref, o_ref, state_ref):
    @pl.when(pl.program_id(1) == 0)
    def _():
        state_ref[...] = jnp.zeros_like(state_ref)

    chunk = RET_CHUNK
    lane_head = lax.broadcasted_iota(jnp.int32, (chunk, RET_QK), 1) // RET_QK_DIM
    state = state_ref[...]
    for ci in range(q_ref.shape[1] // chunk):
        rows = slice(ci * chunk, (ci + 1) * chunk)
        q = q_ref[0, rows, :]
        k = k_ref[0, rows, :]
        v = v_ref[0, rows, :]
        cross = _dot(q, state.astype(BF16)) * xi_ref[...]
        for hd in range(RET_HEADS):
            vcols = slice(hd * RET_V_DIM, (hd + 1) * RET_V_DIM)
            qh = jnp.where(lane_head == hd, q, jnp.zeros_like(q))
            inner = (_dot_nt(qh, k) * dmask_ref[hd]).astype(BF16)
            o = _dot(inner, v[:, vcols]) + cross[:, vcols]
            mu = jnp.mean(o, axis=-1, keepdims=True)
            var = jnp.mean(jnp.square(o - mu), axis=-1, keepdims=True)
            o = (o - mu) * lax.rsqrt(var + EPS)
            o_ref[0, rows, vcols] = (o * sg_ref[0, rows, vcols].astype(F32)).astype(BF16)
        kz = (k.astype(F32) * zeta_ref[...]).astype(BF16)
        state = decay_ref[...] * state + ondiag_ref[...] * _dot_tn(kz, v)
    state_ref[...] = state


def _retention(rq, rk, rv, sg, dmask, xi, zeta, decay, on_diag):
    b, s, _ = rq.shape
    c = RET_CHUNK * RET_STEP_CHUNKS
    tok = lambda width: pl.BlockSpec((1, c, width), lambda bi, ci: (bi, ci, 0))
    const = lambda a: _resident(a.shape, lambda bi, ci: (0,) * a.ndim)
    return pl.pallas_call(
        _retention_kernel,
        grid=(b, s // c),
        in_specs=[tok(RET_QK), tok(RET_QK), tok(RET_WIDTH), tok(RET_WIDTH),
                  const(dmask), const(xi), const(zeta), const(decay), const(on_diag)],
        out_specs=tok(RET_WIDTH),
        out_shape=jax.ShapeDtypeStruct((b, s, RET_WIDTH), BF16),
        scratch_shapes=[pltpu.VMEM((RET_QK, RET_WIDTH), F32)],
        compiler_params=_params("parallel", "arbitrary"),
        name="retention",
    )(rq, rk, rv, sg, dmask, xi, zeta, decay, on_diag)


def _retention_tables():
    c = RET_CHUNK
    log_gamma = jnp.log(1.0 - jnp.power(2.0, -5.0 - jnp.arange(RET_HEADS, dtype=F32)))
    idx = jnp.arange(c, dtype=F32)
    diff = idx[:, None] - idx[None, :]
    dmask = jnp.where(diff >= 0, jnp.exp(log_gamma[:, None, None] * jnp.maximum(diff, 0.0)), 0.0)
    xi = jnp.exp(log_gamma[:, None] * (idx + 1.0))
    zeta = jnp.exp(log_gamma[:, None] * (c - 1.0 - idx))
    g_chunk = jnp.exp(log_gamma * c)
    xi_t = jnp.repeat(xi.T, RET_V_DIM, axis=1)
    zeta_t = jnp.repeat(zeta.T, RET_QK_DIM, axis=1)
    head_of_row = np.arange(RET_QK) // RET_QK_DIM
    head_of_col = np.arange(RET_WIDTH) // RET_V_DIM
    on_diag = jnp.asarray(head_of_row[:, None] == head_of_col[None, :], dtype=F32)
    decay = on_diag * jnp.repeat(g_chunk, RET_QK_DIM)[:, None]
    return dmask, xi_t, zeta_t, decay, on_diag


def _rotation_tables(seq):
    pos = jnp.arange(seq, dtype=F32)
    angle = 1.0 / (10000.0 ** jnp.linspace(0.0, 1.0, RET_QK_DIM // 2, dtype=F32))
    angle = jnp.repeat(angle, 2)
    ang = pos[:, None] * angle[None, :]
    return jnp.tile(jnp.cos(ang), (1, RET_HEADS)), jnp.tile(jnp.sin(ang), (1, RET_HEADS))


def _moba_kernel(qt_ref, k_ref, vt_ref, o_ref,
                 kmean_ref, kabs_ref, qh_ref, shift_ref, bound_ref, m_ref, acc_ref, p_ref,
                 s_even_ref, s_odd_ref, top_even_ref, top_odd_ref):
    qi = pl.program_id(1)
    nb = kmean_ref.shape[0]
    blk = MOBA_BLOCK
    tq = qt_ref.shape[2]
    heads = range(MOBA_HEADS)
    vt_rows = [slice(hd * MOBA_VT_ROWS, (hd + 1) * MOBA_VT_ROWS) for hd in heads]
    chunks = [slice(r, r + MOBA_ROW_CHUNK) for r in range(0, blk, MOBA_ROW_CHUNK)]

    @pl.when(qi == 0)
    def _():
        for j in range(nb):
            kb = k_ref[0, j * blk:(j + 1) * blk, :].astype(F32)
            kmean_ref[j:j + 1, :] = jnp.mean(kb, axis=0, keepdims=True)
            kabs_ref[j:j + 1, :] = jnp.max(jnp.abs(kb), axis=0, keepdims=True)

    qt = qt_ref[0]
    row_head = lax.broadcasted_iota(jnp.int32, (MOBA_WIDTH, tq), 0) // MOBA_HEAD_DIM
    for hd in heads:
        qh_ref[hd] = jnp.where(row_head == hd, qt, jnp.zeros_like(qt))

    kmean = kmean_ref[...].astype(BF16)
    kabs = (kabs_ref[...] * MOBA_BOUND_SLACK).astype(BF16)
    blk_id = lax.broadcasted_iota(jnp.int32, (nb, tq), 0)
    past = blk_id < qi
    for hd in heads:
        gate = jnp.where(past, _dot(kmean, qh_ref[hd]), NEG_INF)
        chosen = jnp.zeros((nb, tq), jnp.bool_)
        for _ in range(MOBA_TOPK):
            best = jnp.max(gate, axis=0, keepdims=True)
            idx = jnp.min(jnp.where(gate == best, blk_id, nb), axis=0, keepdims=True)
            hit = blk_id == idx
            chosen = jnp.logical_or(chosen, jnp.logical_and(hit, past))
            gate = jnp.where(hit, NEG_INF, gate)
        score_bound = _dot(kabs, jnp.abs(qh_ref[hd]))
        attended = jnp.logical_or(chosen, blk_id == qi)
        bound = jnp.max(jnp.where(attended, score_bound, NEG_INF), axis=0, keepdims=True)
        bound_ref[hd:hd + 1, :] = bound
        shift_ref[hd] = jnp.where(chosen, bound, float("inf"))

    def causal(s, row0):
        key_pos = row0 + lax.broadcasted_iota(jnp.int32, s.shape, 0)
        qry_pos = lax.broadcasted_iota(jnp.int32, s.shape, 1)
        return jnp.where(key_pos <= qry_pos, s, NEG_INF)

    def key_block(j):
        return k_ref[0, pl.ds(pl.multiple_of(j * blk, blk), blk), :]

    def finish():
        out = []
        for hd in heads:
            base = hd * MOBA_VT_ROWS
            denom = acc_ref[base + MOBA_HEAD_DIM:base + MOBA_HEAD_DIM + 1, :]
            out.append(acc_ref[base:base + MOBA_HEAD_DIM, :] / denom)
        o_ref[0] = jnp.concatenate(out, axis=0).T.astype(BF16)

    def stage_scores(j, s_ref, heads):
        kj = key_block(j)
        for hd in heads:
            s_ref[hd] = _dot(kj, qh_ref[hd])

    def stage_values(j, s_ref, slot, heads, own=False):
        for hd in heads:
            shift = bound_ref[hd:hd + 1, :] if own else shift_ref[hd, pl.ds(j, 1), :]
            for c in chunks:
                s = s_ref[hd, c, :]
                s = causal(s, c.start) if own else s
                p_ref[slot, hd, c, :] = jnp.exp2(s - shift).astype(BF16)
            pv = _dot(vt_ref[0, j, vt_rows[hd], :], p_ref[slot, hd])
            acc_ref[vt_rows[hd], :] = acc_ref[vt_rows[hd], :] + pv

    acc_ref[...] = jnp.zeros_like(acc_ref)
    head_groups = [(0, 1), (2, 3)]
    stage_scores(0, s_even_ref, heads)

    def fast_body(t, carry):
        j = 2 * t
        for g in head_groups:
            stage_scores(j + 1, s_odd_ref, g)
            stage_values(j, s_even_ref, 0, g)
        for g in head_groups:
            stage_scores(jnp.minimum(j + 2, qi), s_even_ref, g)
            stage_values(j + 1, s_odd_ref, 1, g)
        return carry

    lax.fori_loop(0, (qi + 1) // 2, fast_body, 0)
    stage_values(qi, s_even_ref, 0, heads, own=True)

    sums = [acc_ref[hd * MOBA_VT_ROWS + MOBA_HEAD_DIM:hd * MOBA_VT_ROWS + MOBA_HEAD_DIM + 1, :]
            for hd in heads]
    in_range = jnp.min(jnp.concatenate(sums, axis=0)) >= MOBA_MIN_SUM

    @pl.when(in_range)
    def _():
        finish()

    def col_max(tiles):
        top = tiles[0]
        for t in tiles[1:]:
            top = jnp.maximum(top, t)
        return jnp.max(top, axis=0, keepdims=True)

    def scores(j, buf, heads=heads):
        s_ref, top_ref = buf
        kj = key_block(j)
        for hd in heads:
            s = _dot(kj, qh_ref[hd])
            s_ref[hd] = s
            top_ref[hd:hd + 1, :] = col_max([s[c] for c in chunks])

    def attend(j, buf, own, heads=heads):
        s_ref, top_ref = buf
        for hd in heads:
            def chunk(c):
                s = s_ref[hd, c, :]
                return causal(s, c.start) if own else s

            m_old = m_ref[hd:hd + 1, :]
            if own:
                m_new = jnp.maximum(m_old, col_max([chunk(c) for c in chunks]))
                shift = m_new
            else:
                pen = shift_ref[hd, pl.ds(j, 1), :] - bound_ref[hd:hd + 1, :]
                m_new = jnp.maximum(m_old, top_ref[hd:hd + 1, :] - pen)
                shift = m_new + pen
            m_ref[hd:hd + 1, :] = m_new
            for c in chunks:
                p_ref[0, hd, c, :] = jnp.exp2((chunk(c) - shift).astype(BF16))
            pv = _dot(vt_ref[0, j, vt_rows[hd], :], p_ref[0, hd])
            acc_ref[vt_rows[hd], :] = jnp.exp2(m_old - m_new) * acc_ref[vt_rows[hd], :] + pv

    @pl.when(jnp.logical_not(in_range))
    def _():
        m_ref[...] = jnp.full(m_ref.shape, MOBA_M_INIT, F32)
        acc_ref[...] = jnp.zeros_like(acc_ref)
        even = (s_even_ref, top_even_ref)
        odd = (s_odd_ref, top_odd_ref)
        scores(0, even)

        def body(t, carry):
            j = 2 * t
            scores(j + 1, odd)
            attend(j, even, own=False)
            scores(jnp.minimum(j + 2, qi), even)
            attend(j + 1, odd, own=False)
            return carry

        lax.fori_loop(0, (qi + 1) // 2, body, 0)
        attend(qi, even, own=True)
        finish()


def _moba(mqt, mk, mvt):
    b, s, _ = mk.shape
    nb = s // MOBA_BLOCK
    tq = MOBA_BLOCK
    vt_rows = MOBA_HEADS * MOBA_VT_ROWS
    block_stat_buf = pltpu.VMEM((nb, MOBA_WIDTH), F32)
    scores_buf = pltpu.VMEM((MOBA_HEADS, MOBA_BLOCK, tq), F32)
    head_rows_buf = pltpu.VMEM((8, tq), F32)
    return pl.pallas_call(
        _moba_kernel,
        grid=(b, nb),
        in_specs=[
            pl.BlockSpec((1, MOBA_WIDTH, tq), lambda bi, qi: (bi, 0, qi)),
            pl.BlockSpec((1, s, MOBA_WIDTH), lambda bi, qi: (bi, 0, 0)),
            pl.BlockSpec((1, nb, vt_rows, MOBA_BLOCK), lambda bi, qi: (bi, 0, 0, 0)),
        ],
        out_specs=pl.BlockSpec((1, tq, MOBA_WIDTH), lambda bi, qi: (bi, qi, 0)),
        out_shape=jax.ShapeDtypeStruct((b, s, MOBA_WIDTH), BF16),
        scratch_shapes=[
            block_stat_buf, block_stat_buf,
            pltpu.VMEM((MOBA_HEADS, MOBA_WIDTH, tq), BF16),
            pltpu.VMEM((MOBA_HEADS, nb, tq), F32),
            head_rows_buf,
            head_rows_buf,
            pltpu.VMEM((vt_rows, tq), F32),
            pltpu.VMEM((2, MOBA_HEADS, MOBA_BLOCK, tq), BF16),
            scores_buf, scores_buf,
            head_rows_buf, head_rows_buf,
        ],
        compiler_params=_params("parallel", "arbitrary"),
        name="moba",
    )(mqt, mk, mvt)


def _tail_kernel(x_ref, yr_ref, yp_ref, ym_ref, p_ref, wout_ref,
                 ffn_gain_ref, wg_ref, wu_ref, wd_ref,
                 ple_gain_ref, ple_wgate_ref, ple_wproj_ref, final_gain_ref,
                 o_ref, act_ref, *, final):
    x = x_ref[...]
    x = x + _dot(yr_ref[...], wout_ref[:RET_WIDTH, :])
    x = x + _dot(yp_ref[...], wout_ref[RET_WIDTH:RET_WIDTH + POOL_WIDTH, :])
    x = x + _dot(ym_ref[...], wout_ref[RET_WIDTH + POOL_WIDTH:, :])
    x = _swiglu_half_step(x, ffn_gain_ref[...], wg_ref, wu_ref, wd_ref, act_ref)
    h = _rms(x, ple_gain_ref[...]).astype(BF16)
    gate = jax.nn.sigmoid(_dot(h, ple_wgate_ref[...]))
    x = x + gate * _dot(p_ref[...].astype(BF16), ple_wproj_ref[...])
    if final:
        x = _rms(x, final_gain_ref[...])
    o_ref[...] = x


def _tail(x, y_ret, y_pool, y_moba, p, w_out, ffn, ple_gain, ple_w_gate, ple_w_proj,
          final_gain, layer, final):
    tokens = x.shape[0]
    tm = TOKEN_TILE
    ffn_gain, wg, wu, wd = ffn
    tok = lambda width: pl.BlockSpec((tm, width), lambda i: (i, 0))
    per_layer = lambda *shape: _resident((None,) + shape, lambda i: (layer,) + (0,) * len(shape))
    return pl.pallas_call(
        functools.partial(_tail_kernel, final=final),
        grid=(tokens // tm,),
        in_specs=[
            tok(D_MODEL), tok(RET_WIDTH), tok(POOL_WIDTH), tok(MOBA_WIDTH),
            pl.BlockSpec((None, tm, PLE_DIM), lambda i: (layer, i, 0)),
            per_layer(D_MODEL, D_MODEL),
            per_layer(1, D_MODEL), per_layer(D_MODEL, D_FF), per_layer(D_MODEL, D_FF),
            per_layer(D_FF, D_MODEL),
            per_layer(1, D_MODEL), per_layer(D_MODEL, D_MODEL), per_layer(PLE_DIM, D_MODEL),
            _resident((1, D_MODEL), lambda i: (0, 0)),
        ],
        out_specs=tok(D_MODEL),
        out_shape=jax.ShapeDtypeStruct(x.shape, F32),
        scratch_shapes=[pltpu.VMEM((tm, D_FF), BF16)],
        compiler_params=_params("parallel"),
        name="tail",
    )(x, y_ret, y_pool, y_moba, p, w_out, ffn_gain, wg, wu, wd,
      ple_gain, ple_w_gate, ple_w_proj, final_gain)


def _block_diag(w):
    depth, groups, d, _ = w.shape
    eye = jnp.eye(groups, dtype=w.dtype)
    return jnp.einsum("lgcd,gh->lgchd", w, eye).reshape(depth, groups * d, groups * d)


def kernel(x, p, norm_ffn1, ffn1_w_gate, ffn1_w_up, ffn1_w_down, norm_mix, w_in, pool_w, pool_scale, w_out, norm_ffn2, ffn2_w_gate, ffn2_w_up, ffn2_w_down, norm_ple, ple_w_gate, ple_w_proj, norm_final):
    b, s, d = x.shape
    depth = w_in.shape[0]
    tokens = b * s
    bf = lambda w: w.astype(BF16)
    row = lambda g: g.reshape(g.shape[0], 1, g.shape[1])

    ffn1 = (row(norm_ffn1), bf(ffn1_w_gate), bf(ffn1_w_up), bf(ffn1_w_down))
    ffn2 = (row(norm_ffn2), bf(ffn2_w_gate), bf(ffn2_w_up), bf(ffn2_w_down))
    w_in_b, w_out_b = bf(w_in), bf(w_out)
    pool_w_b = bf(_block_diag(pool_w))
    ple_gate_b, ple_proj_b = bf(ple_w_gate), bf(ple_w_proj)
    norm_mix_r, norm_ple_r, pool_scale_r = row(norm_mix), row(norm_ple), row(pool_scale)
    final_gain = norm_final.reshape(1, d)
    p_flat = p.reshape(depth, tokens, PLE_DIM)
    cos, sin = _rotation_tables(s)
    ret_tables = _retention_tables()

    xf = x.reshape(tokens, d)
    for i in range(depth):
        xf = _ffn(xf, *ffn1, i)
        rq, rk, rv, sg, y_pool, mqt, mk, mvt = _proj(
            xf.reshape(b, s, d), norm_mix_r, w_in_b, cos, sin, pool_w_b, pool_scale_r, i)
        y_ret = _retention(rq, rk, rv, sg, *ret_tables)
        y_moba = _moba(mqt, mk, mvt)
        xf = _tail(xf, y_ret.reshape(tokens, RET_WIDTH), y_pool.reshape(tokens, POOL_WIDTH),
                   y_moba.reshape(tokens, MOBA_WIDTH), p_flat, w_out_b, ffn2,
                   norm_ple_r, ple_gate_b, ple_proj_b, final_gain, i, i == depth - 1)
    return xf.reshape(b, s, d)
```

```python
import functools

import jax
import jax.numpy as jnp
import numpy as np
from jax import lax
from jax.experimental import pallas as pl
from jax.experimental.pallas import tpu as pltpu

D_MODEL = 1024
D_FF = 2816
RET_HEADS = 4
RET_QK_DIM = 64
RET_V_DIM = 128
RET_CHUNK = 256
RET_STEP_CHUNKS = 8
RET_QK = RET_HEADS * RET_QK_DIM
RET_WIDTH = RET_HEADS * RET_V_DIM
POOL_GROUPS = 4
POOL_WINDOWS = (2, 4, 8, 16)
POOL_GROUP_DIM = 64
POOL_WIDTH = POOL_GROUPS * POOL_GROUP_DIM
POOL_HALO = 16
assert POOL_WINDOWS == tuple(2 ** (i + 1) for i in range(POOL_GROUPS)) and POOL_HALO == POOL_WINDOWS[-1]
MOBA_HEADS = 4
MOBA_HEAD_DIM = 64
MOBA_WIDTH = MOBA_HEADS * MOBA_HEAD_DIM
MOBA_BLOCK = 256
MOBA_TOPK = 3
MOBA_VT_ROWS = MOBA_HEAD_DIM + 16
MOBA_ROW_CHUNK = 64
MOBA_M_INIT = -(2.0 ** 100)
MOBA_TRIP_BLOCKS = 4
MOBA_BOUND_SLACK = 1.0 + 2.0 ** -7
MOBA_MIN_SUM = 2.0 ** -100
PLE_DIM = 256
EPS = 1e-6

FF_CHUNK = 256
TOKEN_TILE = 512
VMEM_LIMIT = 56 * 1024 * 1024

BF16 = jnp.bfloat16
F32 = jnp.float32
NEG_INF = float("-inf")
LOG2_E = 1.4426950408889634


def _params(*semantics):
    return pltpu.CompilerParams(dimension_semantics=semantics, vmem_limit_bytes=VMEM_LIMIT)


def _resident(shape, index_map):
    return pl.BlockSpec(shape, index_map, pipeline_mode=pl.Buffered(1))


def _rms(x, gain):
    return x * lax.rsqrt(jnp.mean(x * x, axis=-1, keepdims=True) + EPS) * gain


def _dot(a, b):
    return jnp.dot(a, b, preferred_element_type=F32)


def _dot_nt(a, b):
    return lax.dot_general(a, b, (((1,), (1,)), ((), ())), preferred_element_type=F32)


def _dot_tn(a, b):
    return lax.dot_general(a, b, (((0,), (0,)), ((), ())), preferred_element_type=F32)


def _swiglu_half_step(x, gain, wg_ref, wu_ref, wd_ref, act_ref):
    h = _rms(x, gain).astype(BF16)
    for c in range(D_FF // FF_CHUNK):
        cols = slice(c * FF_CHUNK, (c + 1) * FF_CHUNK)
        gate = _dot(h, wg_ref[:, cols])
        up = _dot(h, wu_ref[:, cols])
        act_ref[:, cols] = (gate * jax.nn.sigmoid(gate) * up).astype(BF16)
    return x + 0.5 * _dot(act_ref[...], wd_ref[...])


def _ffn_kernel(x_ref, gain_ref, wg_ref, wu_ref, wd_ref, o_ref, act_ref):
    o_ref[...] = _swiglu_half_step(x_ref[...], gain_ref[...], wg_ref, wu_ref, wd_ref, act_ref)


def _ffn(x, gain, wg, wu, wd, layer):
    tokens = x.shape[0]
    tm = TOKEN_TILE
    return pl.pallas_call(
        _ffn_kernel,
        grid=(tokens // tm,),
        in_specs=[
            pl.BlockSpec((tm, D_MODEL), lambda i: (i, 0)),
            _resident((None, 1, D_MODEL), lambda i: (layer, 0, 0)),
            _resident((None, D_MODEL, D_FF), lambda i: (layer, 0, 0)),
            _resident((None, D_MODEL, D_FF), lambda i: (layer, 0, 0)),
            _resident((None, D_FF, D_MODEL), lambda i: (layer, 0, 0)),
        ],
        out_specs=pl.BlockSpec((tm, D_MODEL), lambda i: (i, 0)),
        out_shape=jax.ShapeDtypeStruct(x.shape, F32),
        scratch_shapes=[pltpu.VMEM((tm, D_FF), BF16)],
        compiler_params=_params("parallel"),
        name="ffn",
    )(x, gain, wg, wu, wd)


def _rotate_every_two(t, even_lane):
    n = t.shape[-1]
    nxt = pltpu.roll(t, n - 1, 1)
    prv = pltpu.roll(t, 1, 1)
    return jnp.where(even_lane, -nxt, prv)


def _proj_kernel(x_ref, gain_ref, w_ref, cos_ref, sin_ref, poolw_ref, poolscale_ref,
                 rq_ref, rk_ref, rv_ref, sg_ref, ypool_ref, mqt_ref, mk_ref, mvt_ref,
                 halo_ref):
    tm = x_ref.shape[1]
    st = pl.program_id(1)
    h = _rms(x_ref[0], gain_ref[...]).astype(BF16)

    def cols(lo, width):
        return _dot(h, w_ref[:, lo:lo + width])

    cos = cos_ref[...]
    sin = sin_ref[...]
    even_lane = (lax.broadcasted_iota(jnp.int32, (tm, RET_QK), 1) % 2) == 0
    q = cols(0, RET_QK)
    rq_ref[0] = (q * cos + _rotate_every_two(q, even_lane) * sin).astype(BF16)
    k = cols(RET_QK, RET_QK)
    k = (k * cos + _rotate_every_two(k, even_lane) * sin) * (RET_QK_DIM ** -0.5)
    rk_ref[0] = k.astype(BF16)
    rv_ref[0] = cols(2 * RET_QK, RET_WIDTH).astype(BF16)
    g = cols(2 * RET_QK + RET_WIDTH, RET_WIDTH)
    sg_ref[0] = (g * jax.nn.sigmoid(g)).astype(BF16)

    off = 2 * RET_QK + 2 * RET_WIDTH
    u = cols(off, POOL_WIDTH)

    first_tile = st == 0
    halo = jnp.where(first_tile, 0.0, halo_ref[...])
    ext = jnp.concatenate([halo, u], axis=0)
    halo_ref[...] = u[tm - POOL_HALO:, :]
    group = lax.broadcasted_iota(jnp.int32, (tm, POOL_WIDTH), 1) // POOL_GROUP_DIM
    group_row = lax.broadcasted_iota(jnp.int32, (1, POOL_WIDTH), 1) // POOL_GROUP_DIM
    wsum = None
    window = None
    acc = ext
    for gi, win in enumerate(POOL_WINDOWS):
        acc = acc + pltpu.roll(acc, win // 2, 0)
        cur = acc[POOL_HALO:, :]
        wsum = cur if wsum is None else jnp.where(group == gi, cur, wsum)
        wl = jnp.full((1, POOL_WIDTH), float(win), F32)
        window = wl if window is None else jnp.where(group_row == gi, wl, window)

    pos = (lax.broadcasted_iota(jnp.int32, (POOL_HALO, POOL_WIDTH), 0) + 1).astype(F32)
    short = jnp.where(first_tile, jnp.minimum(pos, window), window)
    pooled = jnp.concatenate([wsum[:POOL_HALO] / short, wsum[POOL_HALO:] * (1.0 / window)],
                             axis=0) - u

    off += POOL_WIDTH
    mq = cols(off, MOBA_WIDTH) * (MOBA_HEAD_DIM ** -0.5 * LOG2_E)
    mqt_ref[0] = mq.T.astype(BF16)
    mk_ref[0] = cols(off + MOBA_WIDTH, MOBA_WIDTH).astype(BF16)
    mv = cols(off + 2 * MOBA_WIDTH, MOBA_WIDTH)
    ones = jnp.ones((MOBA_VT_ROWS - MOBA_HEAD_DIM, MOBA_BLOCK), F32)
    for j in range(tm // MOBA_BLOCK):
        mvt = mv[j * MOBA_BLOCK:(j + 1) * MOBA_BLOCK, :].T
        pieces = []
        for hd in range(MOBA_HEADS):
            pieces += [mvt[hd * MOBA_HEAD_DIM:(hd + 1) * MOBA_HEAD_DIM, :], ones]
        mvt_ref[0, j] = jnp.concatenate(pieces, axis=0).astype(BF16)

    y_pool = _dot(pooled.astype(BF16), poolw_ref[...]) * poolscale_ref[...]
    ypool_ref[0] = y_pool.astype(BF16)


def _proj(x, gain, w_in, cos, sin, pool_w, pool_scale, layer):
    b, s, _ = x.shape
    tm = TOKEN_TILE
    in_cols = w_in.shape[-1]
    nb = s // MOBA_BLOCK
    bpt = tm // MOBA_BLOCK
    tok = lambda width: pl.BlockSpec((1, tm, width), lambda bi, si: (bi, si, 0))
    out_shape = [
        jax.ShapeDtypeStruct((b, s, RET_QK), BF16),
        jax.ShapeDtypeStruct((b, s, RET_QK), BF16),
        jax.ShapeDtypeStruct((b, s, RET_WIDTH), BF16),
        jax.ShapeDtypeStruct((b, s, RET_WIDTH), BF16),
        jax.ShapeDtypeStruct((b, s, POOL_WIDTH), BF16),
        jax.ShapeDtypeStruct((b, MOBA_WIDTH, s), BF16),
        jax.ShapeDtypeStruct((b, s, MOBA_WIDTH), BF16),
        jax.ShapeDtypeStruct((b, nb, MOBA_HEADS * MOBA_VT_ROWS, MOBA_BLOCK), BF16),
    ]
    out_specs = [
        tok(RET_QK), tok(RET_QK), tok(RET_WIDTH), tok(RET_WIDTH), tok(POOL_WIDTH),
        pl.BlockSpec((1, MOBA_WIDTH, tm), lambda bi, si: (bi, 0, si)),
        tok(MOBA_WIDTH),
        pl.BlockSpec((1, bpt, MOBA_HEADS * MOBA_VT_ROWS, MOBA_BLOCK), lambda bi, si: (bi, si, 0, 0)),
    ]
    return pl.pallas_call(
        _proj_kernel,
        grid=(b, s // tm),
        in_specs=[
            pl.BlockSpec((1, tm, D_MODEL), lambda bi, si: (bi, si, 0)),
            _resident((None, 1, D_MODEL), lambda bi, si: (layer, 0, 0)),
            _resident((None, D_MODEL, in_cols), lambda bi, si: (layer, 0, 0)),
            pl.BlockSpec((tm, RET_QK), lambda bi, si: (si, 0)),
            pl.BlockSpec((tm, RET_QK), lambda bi, si: (si, 0)),
            _resident((None, POOL_WIDTH, POOL_WIDTH), lambda bi, si: (layer, 0, 0)),
            _resident((None, 1, POOL_WIDTH), lambda bi, si: (layer, 0, 0)),
        ],
        out_specs=out_specs,
        out_shape=out_shape,
        scratch_shapes=[pltpu.VMEM((POOL_HALO, POOL_WIDTH), F32)],
        compiler_params=_params("parallel", "arbitrary"),
        name="proj",
    )(x, gain, w_in, cos, sin, pool_w, pool_scale)


def _retention_kernel(q_ref, k_ref, v_ref, sg_ref, dmask_ref, xi_ref, zeta_ref, decay_ref,
                      ondiag_ref, o_ref, state_ref):
    @pl.when(pl.program_id(1) == 0)
    def _():
        state_ref[...] = jnp.zeros_like(state_ref)

    chunk = RET_CHUNK
    lane_head = lax.broadcasted_iota(jnp.int32, (chunk, RET_QK), 1) // RET_QK_DIM
    state = state_ref[...]
    for ci in range(q_ref.shape[1] // chunk):
        rows = slice(ci * chunk, (ci + 1) * chunk)
        q = q_ref[0, rows, :]
        k = k_ref[0, rows, :]
        v = v_ref[0, rows, :]
        cross = _dot(q, state.astype(BF16)) * xi_ref[...]
        for hd in range(RET_HEADS):
            vcols = slice(hd * RET_V_DIM, (hd + 1) * RET_V_DIM)
            qh = jnp.where(lane_head == hd, q, jnp.zeros_like(q))
            inner = (_dot_nt(qh, k) * dmask_ref[hd]).astype(BF16)
            o = _dot(inner, v[:, vcols]) + cross[:, vcols]
            mu = jnp.mean(o, axis=-1, keepdims=True)
            var = jnp.mean(jnp.square(o - mu), axis=-1, keepdims=True)
            o = (o - mu) * lax.rsqrt(var + EPS)
            o_ref[0, rows, vcols] = (o * sg_ref[0, rows, vcols].astype(F32)).astype(BF16)
        kz = (k.astype(F32) * zeta_ref[...]).astype(BF16)
        state = decay_ref[...] * state + ondiag_ref[...] * _dot_tn(kz, v)
    state_ref[...] = state


def _retention(rq, rk, rv, sg, dmask, xi, zeta, decay, on_diag):
    b, s, _ = rq.shape
    c = RET_CHUNK * RET_STEP_CHUNKS
    tok = lambda width: pl.BlockSpec((1, c, width), lambda bi, ci: (bi, ci, 0))
    const = lambda a: _resident(a.shape, lambda bi, ci: (0,) * a.ndim)
    return pl.pallas_call(
        _retention_kernel,
        grid=(b, s // c),
        in_specs=[tok(RET_QK), tok(RET_QK), tok(RET_WIDTH), tok(RET_WIDTH),
                  const(dmask), const(xi), const(zeta), const(decay), const(on_diag)],
        out_specs=tok(RET_WIDTH),
        out_shape=jax.ShapeDtypeStruct((b, s, RET_WIDTH), BF16),
        scratch_shapes=[pltpu.VMEM((RET_QK, RET_WIDTH), F32)],
        compiler_params=_params("parallel", "arbitrary"),
        name="retention",
    )(rq, rk, rv, sg, dmask, xi, zeta, decay, on_diag)


def _retention_tables():
    c = RET_CHUNK
    log_gamma = jnp.log(1.0 - jnp.power(2.0, -5.0 - jnp.arange(RET_HEADS, dtype=F32)))
    idx = jnp.arange(c, dtype=F32)
    diff = idx[:, None] - idx[None, :]
    dmask = jnp.where(diff >= 0, jnp.exp(log_gamma[:, None, None] * jnp.maximum(diff, 0.0)), 0.0)
    xi = jnp.exp(log_gamma[:, None] * (idx + 1.0))
    zeta = jnp.exp(log_gamma[:, None] * (c - 1.0 - idx))
    g_chunk = jnp.exp(log_gamma * c)
    xi_t = jnp.repeat(xi.T, RET_V_DIM, axis=1)
    zeta_t = jnp.repeat(zeta.T, RET_QK_DIM, axis=1)
    head_of_row = np.arange(RET_QK) // RET_QK_DIM
    head_of_col = np.arange(RET_WIDTH) // RET_V_DIM
    on_diag = jnp.asarray(head_of_row[:, None] == head_of_col[None, :], dtype=F32)
    decay = on_diag * jnp.repeat(g_chunk, RET_QK_DIM)[:, None]
    return dmask, xi_t, zeta_t, decay, on_diag


def _rotation_tables(seq):
    pos = jnp.arange(seq, dtype=F32)
    angle = 1.0 / (10000.0 ** jnp.linspace(0.0, 1.0, RET_QK_DIM // 2, dtype=F32))
    angle = jnp.repeat(angle, 2)
    ang = pos[:, None] * angle[None, :]
    return jnp.tile(jnp.cos(ang), (1, RET_HEADS)), jnp.tile(jnp.sin(ang), (1, RET_HEADS))


def _moba_kernel(qt_ref, k_ref, vt_ref, o_ref,
                 kmean_ref, kabs_ref, qh_ref, shift_ref, bound_ref, m_ref, acc_ref, p_ref,
                 s_even_ref, s_odd_ref, top_even_ref, top_odd_ref):
    qi = pl.program_id(1)
    nb = kmean_ref.shape[0]
    blk = MOBA_BLOCK
    tq = qt_ref.shape[2]
    heads = range(MOBA_HEADS)
    vt_rows = [slice(hd * MOBA_VT_ROWS, (hd + 1) * MOBA_VT_ROWS) for hd in heads]
    chunks = [slice(r, r + MOBA_ROW_CHUNK) for r in range(0, blk, MOBA_ROW_CHUNK)]

    @pl.when(qi == 0)
    def _():
        for j in range(nb):
            kb = k_ref[0, j * blk:(j + 1) * blk, :].astype(F32)
            kmean_ref[j:j + 1, :] = jnp.mean(kb, axis=0, keepdims=True)
            kabs_ref[j:j + 1, :] = jnp.max(jnp.abs(kb), axis=0, keepdims=True)

    qt = qt_ref[0]
    row_head = lax.broadcasted_iota(jnp.int32, (MOBA_WIDTH, tq), 0) // MOBA_HEAD_DIM
    for hd in heads:
        qh_ref[hd] = jnp.where(row_head == hd, qt, jnp.zeros_like(qt))

    kmean = kmean_ref[...].astype(BF16)
    kabs = (kabs_ref[...] * MOBA_BOUND_SLACK).astype(BF16)
    blk_id = lax.broadcasted_iota(jnp.int32, (nb, tq), 0)
    past = blk_id < qi
    for hd in heads:
        gate = jnp.where(past, _dot(kmean, qh_ref[hd]), NEG_INF)
        chosen = jnp.zeros((nb, tq), jnp.bool_)
        for _ in range(MOBA_TOPK):
            best = jnp.max(gate, axis=0, keepdims=True)
            idx = jnp.min(jnp.where(gate == best, blk_id, nb), axis=0, keepdims=True)
            hit = blk_id == idx
            chosen = jnp.logical_or(chosen, jnp.logical_and(hit, past))
            gate = jnp.where(hit, NEG_INF, gate)
        score_bound = _dot(kabs, jnp.abs(qh_ref[hd]))
        attended = jnp.logical_or(chosen, blk_id == qi)
        bound = jnp.max(jnp.where(attended, score_bound, NEG_INF), axis=0, keepdims=True)
        bound_ref[hd:hd + 1, :] = bound
        shift_ref[hd] = jnp.where(chosen, bound, float("inf"))

    def causal(s, row0):
        key_pos = row0 + lax.broadcasted_iota(jnp.int32, s.shape, 0)
        qry_pos = lax.broadcasted_iota(jnp.int32, s.shape, 1)
        return jnp.where(key_pos <= qry_pos, s, NEG_INF)

    def key_block(j):
        return k_ref[0, pl.ds(pl.multiple_of(j * blk, blk), blk), :]

    def finish():
        out = []
        for hd in heads:
            base = hd * MOBA_VT_ROWS
            denom = acc_ref[base + MOBA_HEAD_DIM:base + MOBA_HEAD_DIM + 1, :]
            out.append(acc_ref[base:base + MOBA_HEAD_DIM, :] / denom)
        o_ref[0] = jnp.concatenate(out, axis=0).T.astype(BF16)

    def stage_scores(j, s_ref, heads):
        kj = key_block(j)
        for hd in heads:
            s_ref[hd] = _dot(kj, qh_ref[hd])

    def stage_values(j, s_ref, slot, heads, own=False):
        for hd in heads:
            shift = bound_ref[hd:hd + 1, :] if own else shift_ref[hd, pl.ds(j, 1), :]
            for c in chunks:
                s = s_ref[hd, c, :]
                s = causal(s, c.start) if own else s
                p_ref[slot, hd, c, :] = jnp.exp2(s - shift).astype(BF16)
            pv = _dot(vt_ref[0, j, vt_rows[hd], :], p_ref[slot, hd])
            acc_ref[vt_rows[hd], :] = acc_ref[vt_rows[hd], :] + pv

    acc_ref[...] = jnp.zeros_like(acc_ref)
    head_groups = [(0, 1), (2, 3)]
    stage_scores(0, s_even_ref, heads)

    def fast_body(t, carry):
        for i in range(MOBA_TRIP_BLOCKS):
            j = jnp.minimum(MOBA_TRIP_BLOCKS * t + i, qi)
            cur, nxt = (s_even_ref, s_odd_ref) if i % 2 == 0 else (s_odd_ref, s_even_ref)
            for g in head_groups:
                stage_scores(jnp.minimum(j + 1, qi), nxt, g)
                stage_values(j, cur, i % 2, g)
        return carry

    lax.fori_loop(0, (qi + MOBA_TRIP_BLOCKS - 1) // MOBA_TRIP_BLOCKS, fast_body, 0)
    stage_values(qi, s_even_ref, 0, heads, own=True)

    sums = [acc_ref[hd * MOBA_VT_ROWS + MOBA_HEAD_DIM:hd * MOBA_VT_ROWS + MOBA_HEAD_DIM + 1, :]
            for hd in heads]
    in_range = jnp.min(jnp.concatenate(sums, axis=0)) >= MOBA_MIN_SUM

    @pl.when(in_range)
    def _():
        finish()

    def col_max(tiles):
        top = tiles[0]
        for t in tiles[1:]:
            top = jnp.maximum(top, t)
        return jnp.max(top, axis=0, keepdims=True)

    def scores(j, buf, heads=heads):
        s_ref, top_ref = buf
        kj = key_block(j)
        for hd in heads:
            s = _dot(kj, qh_ref[hd])
            s_ref[hd] = s
            top_ref[hd:hd + 1, :] = col_max([s[c] for c in chunks])

    def attend(j, buf, own, heads=heads):
        s_ref, top_ref = buf
        for hd in heads:
            def chunk(c):
                s = s_ref[hd, c, :]
                return causal(s, c.start) if own else s

            m_old = m_ref[hd:hd + 1, :]
            if own:
                m_new = jnp.maximum(m_old, col_max([chunk(c) for c in chunks]))
                shift = m_new
            else:
                pen = shift_ref[hd, pl.ds(j, 1), :] - bound_ref[hd:hd + 1, :]
                m_new = jnp.maximum(m_old, top_ref[hd:hd + 1, :] - pen)
                shift = m_new + pen
            m_ref[hd:hd + 1, :] = m_new
            for c in chunks:
                p_ref[0, hd, c, :] = jnp.exp2((chunk(c) - shift).astype(BF16))
            pv = _dot(vt_ref[0, j, vt_rows[hd], :], p_ref[0, hd])
            acc_ref[vt_rows[hd], :] = jnp.exp2(m_old - m_new) * acc_ref[vt_rows[hd], :] + pv

    @pl.when(jnp.logical_not(in_range))
    def _():
        m_ref[...] = jnp.full(m_ref.shape, MOBA_M_INIT, F32)
        acc_ref[...] = jnp.zeros_like(acc_ref)
        even = (s_even_ref, top_even_ref)
        odd = (s_odd_ref, top_odd_ref)
        scores(0, even)

        def body(t, carry):
            j = 2 * t
            scores(j + 1, odd)
            attend(j, even, own=False)
            scores(jnp.minimum(j + 2, qi), even)
            attend(j + 1, odd, own=False)
            return carry

        lax.fori_loop(0, (qi + 1) // 2, body, 0)
        attend(qi, even, own=True)
        finish()


def _moba(mqt, mk, mvt):
    b, s, _ = mk.shape
    nb = s // MOBA_BLOCK
    tq = MOBA_BLOCK
    vt_rows = MOBA_HEADS * MOBA_VT_ROWS
    block_stat_buf = pltpu.VMEM((nb, MOBA_WIDTH), F32)
    scores_buf = pltpu.VMEM((MOBA_HEADS, MOBA_BLOCK, tq), F32)
    head_rows_buf = pltpu.VMEM((8, tq), F32)
    return pl.pallas_call(
        _moba_kernel,
        grid=(b, nb),
        in_specs=[
            pl.BlockSpec((1, MOBA_WIDTH, tq), lambda bi, qi: (bi, 0, qi)),
            pl.BlockSpec((1, s, MOBA_WIDTH), lambda bi, qi: (bi, 0, 0)),
            pl.BlockSpec((1, nb, vt_rows, MOBA_BLOCK), lambda bi, qi: (bi, 0, 0, 0)),
        ],
        out_specs=pl.BlockSpec((1, tq, MOBA_WIDTH), lambda bi, qi: (bi, qi, 0)),
        out_shape=jax.ShapeDtypeStruct((b, s, MOBA_WIDTH), BF16),
        scratch_shapes=[
            block_stat_buf, block_stat_buf,
            pltpu.VMEM((MOBA_HEADS, MOBA_WIDTH, tq), BF16),
            pltpu.VMEM((MOBA_HEADS, nb, tq), F32),
            head_rows_buf,
            head_rows_buf,
            pltpu.VMEM((vt_rows, tq), F32),
            pltpu.VMEM((2, MOBA_HEADS, MOBA_BLOCK, tq), BF16),
            scores_buf, scores_buf,
            head_rows_buf, head_rows_buf,
        ],
        compiler_params=_params("parallel", "arbitrary"),
        name="moba",
    )(mqt, mk, mvt)


def _tail_kernel(x_ref, yr_ref, yp_ref, ym_ref, p_ref, wout_ref,
                 ffn_gain_ref, wg_ref, wu_ref, wd_ref,
                 ple_gain_ref, ple_wgate_ref, ple_wproj_ref, final_gain_ref,
                 o_ref, act_ref, *, final):
    x = x_ref[...]
    x = x + _dot(yr_ref[...], wout_ref[:RET_WIDTH, :])
    x = x + _dot(yp_ref[...], wout_ref[RET_WIDTH:RET_WIDTH + POOL_WIDTH, :])
    x = x + _dot(ym_ref[...], wout_ref[RET_WIDTH + POOL_WIDTH:, :])
    emb = _dot(p_ref[...].astype(BF16), ple_wproj_ref[...])
    x = _swiglu_half_step(x, ffn_gain_ref[...], wg_ref, wu_ref, wd_ref, act_ref)
    h = _rms(x, ple_gain_ref[...]).astype(BF16)
    x = x + jax.nn.sigmoid(_dot(h, ple_wgate_ref[...])) * emb
    if final:
        x = _rms(x, final_gain_ref[...])
    o_ref[...] = x


def _tail(x, y_ret, y_pool, y_moba, p, w_out, ffn, ple_gain, ple_w_gate, ple_w_proj,
          final_gain, layer, final):
    tokens = x.shape[0]
    tm = TOKEN_TILE
    ffn_gain, wg, wu, wd = ffn
    tok = lambda width: pl.BlockSpec((tm, width), lambda i: (i, 0))
    per_layer = lambda *shape: _resident((None,) + shape, lambda i: (layer,) + (0,) * len(shape))
    return pl.pallas_call(
        functools.partial(_tail_kernel, final=final),
        grid=(tokens // tm,),
        in_specs=[
            tok(D_MODEL), tok(RET_WIDTH), tok(POOL_WIDTH), tok(MOBA_WIDTH),
            pl.BlockSpec((None, tm, PLE_DIM), lambda i: (layer, i, 0)),
            per_layer(D_MODEL, D_MODEL),
            per_layer(1, D_MODEL), per_layer(D_MODEL, D_FF), per_layer(D_MODEL, D_FF),
            per_layer(D_FF, D_MODEL),
            per_layer(1, D_MODEL), per_layer(D_MODEL, D_MODEL), per_layer(PLE_DIM, D_MODEL),
            _resident((1, D_MODEL), lambda i: (0, 0)),
        ],
        out_specs=tok(D_MODEL),
        out_shape=jax.ShapeDtypeStruct(x.shape, F32),
        scratch_shapes=[pltpu.VMEM((tm, D_FF), BF16)],
        compiler_params=_params("parallel"),
        name="tail",
    )(x, y_ret, y_pool, y_moba, p, w_out, ffn_gain, wg, wu, wd,
      ple_gain, ple_w_gate, ple_w_proj, final_gain)


def _block_diag(w):
    depth, groups, d, _ = w.shape
    eye = jnp.eye(groups, dtype=w.dtype)
    return jnp.einsum("lgcd,gh->lgchd", w, eye).reshape(depth, groups * d, groups * d)


def kernel(x, p, norm_ffn1, ffn1_w_gate, ffn1_w_up, ffn1_w_down, norm_mix, w_in, pool_w, pool_scale, w_out, norm_ffn2, ffn2_w_gate, ffn2_w_up, ffn2_w_down, norm_ple, ple_w_gate, ple_w_proj, norm_final):
    b, s, d = x.shape
    depth = w_in.shape[0]
    tokens = b * s
    bf = lambda w: w.astype(BF16)
    row = lambda g: g.reshape(g.shape[0], 1, g.shape[1])

    ffn1 = (row(norm_ffn1), bf(ffn1_w_gate), bf(ffn1_w_up), bf(ffn1_w_down))
    ffn2 = (row(norm_ffn2), bf(ffn2_w_gate), bf(ffn2_w_up), bf(ffn2_w_down))
    w_in_b, w_out_b = bf(w_in), bf(w_out)
    pool_w_b = bf(_block_diag(pool_w))
    ple_gate_b, ple_proj_b = bf(ple_w_gate), bf(ple_w_proj)
    norm_mix_r, norm_ple_r, pool_scale_r = row(norm_mix), row(norm_ple), row(pool_scale)
    final_gain = norm_final.reshape(1, d)
    p_flat = p.reshape(depth, tokens, PLE_DIM)
    cos, sin = _rotation_tables(s)
    ret_tables = _retention_tables()

    xf = x.reshape(tokens, d)
    for i in range(depth):
        xf = _ffn(xf, *ffn1, i)
        rq, rk, rv, sg, y_pool, mqt, mk, mvt = _proj(
            xf.reshape(b, s, d), norm_mix_r, w_in_b, cos, sin, pool_w_b, pool_scale_r, i)
        y_ret = _retention(rq, rk, rv, sg, *ret_tables)
        y_moba = _moba(mqt, mk, mvt)
        xf = _tail(xf, y_ret.reshape(tokens, RET_WIDTH), y_pool.reshape(tokens, POOL_WIDTH),
                   y_moba.reshape(tokens, MOBA_WIDTH), p_flat, w_out_b, ffn2,
                   norm_ple_r, ple_gate_b, ple_proj_b, final_gain, i, i == depth - 1)
    return xf.reshape(b, s, d)
```

```python
import functools

import jax
import jax.numpy as jnp
import numpy as np
from jax import lax
from jax.experimental import pallas as pl
from jax.experimental.pallas import tpu as pltpu

D_MODEL = 1024
D_FF = 2816
RET_HEADS = 4
RET_QK_DIM = 64
RET_V_DIM = 128
RET_CHUNK = 256
RET_STEP_CHUNKS = 8
RET_QK = RET_HEADS * RET_QK_DIM
RET_WIDTH = RET_HEADS * RET_V_DIM
POOL_GROUPS = 4
POOL_WINDOWS = (2, 4, 8, 16)
POOL_GROUP_DIM = 64
POOL_WIDTH = POOL_GROUPS * POOL_GROUP_DIM
POOL_HALO = 16
assert POOL_WINDOWS == tuple(2 ** (i + 1) for i in range(POOL_GROUPS)) and POOL_HALO == POOL_WINDOWS[-1]
MOBA_HEADS = 4
MOBA_HEAD_DIM = 64
MOBA_WIDTH = MOBA_HEADS * MOBA_HEAD_DIM
MOBA_BLOCK = 256
MOBA_TOPK = 3
MOBA_VT_ROWS = MOBA_HEAD_DIM + 16
MOBA_ROW_CHUNK = 64
MOBA_M_INIT = -(2.0 ** 100)
MOBA_TRIP_BLOCKS = 4
MOBA_BOUND_SLACK = 1.0 + 2.0 ** -7
MOBA_MIN_SUM = 2.0 ** -100
PLE_DIM = 256
EPS = 1e-6

FF_CHUNK = 256
TOKEN_TILE = 512
VMEM_LIMIT = 56 * 1024 * 1024

BF16 = jnp.bfloat16
F32 = jnp.float32
NEG_INF = float("-inf")
LOG2_E = 1.4426950408889634


def _params(*semantics):
    return pltpu.CompilerParams(dimension_semantics=semantics, vmem_limit_bytes=VMEM_LIMIT)


def _resident(shape, index_map):
    return pl.BlockSpec(shape, index_map, pipeline_mode=pl.Buffered(1))


def _rms(x, gain):
    return x * lax.rsqrt(jnp.mean(x * x, axis=-1, keepdims=True) + EPS) * gain


def _dot(a, b):
    return jnp.dot(a, b, preferred_element_type=F32)


def _dot_nt(a, b):
    return lax.dot_general(a, b, (((1,), (1,)), ((), ())), preferred_element_type=F32)


def _dot_tn(a, b):
    return lax.dot_general(a, b, (((0,), (0,)), ((), ())), preferred_element_type=F32)


def _swiglu_half_step(x, gain, wg_ref, wu_ref, wd_ref, act_ref):
    h = _rms(x, gain).astype(BF16)
    for c in range(D_FF // FF_CHUNK):
        cols = slice(c * FF_CHUNK, (c + 1) * FF_CHUNK)
        gate = _dot(h, wg_ref[:, cols])
        up = _dot(h, wu_ref[:, cols])
        act_ref[:, cols] = (gate * jax.nn.sigmoid(gate) * up).astype(BF16)
    return x + 0.5 * _dot(act_ref[...], wd_ref[...])


def _ffn_kernel(x_ref, gain_ref, wg_ref, wu_ref, wd_ref, o_ref, act_ref):
    o_ref[...] = _swiglu_half_step(x_ref[...], gain_ref[...], wg_ref, wu_ref, wd_ref, act_ref)


def _ffn(x, gain, wg, wu, wd, layer):
    tokens = x.shape[0]
    tm = TOKEN_TILE
    return pl.pallas_call(
        _ffn_kernel,
        grid=(tokens // tm,),
        in_specs=[
            pl.BlockSpec((tm, D_MODEL), lambda i: (i, 0)),
            _resident((None, 1, D_MODEL), lambda i: (layer, 0, 0)),
            _resident((None, D_MODEL, D_FF), lambda i: (layer, 0, 0)),
            _resident((None, D_MODEL, D_FF), lambda i: (layer, 0, 0)),
            _resident((None, D_FF, D_MODEL), lambda i: (layer, 0, 0)),
        ],
        out_specs=pl.BlockSpec((tm, D_MODEL), lambda i: (i, 0)),
        out_shape=jax.ShapeDtypeStruct(x.shape, F32),
        scratch_shapes=[pltpu.VMEM((tm, D_FF), BF16)],
        compiler_params=_params("parallel"),
        name="ffn",
    )(x, gain, wg, wu, wd)


def _rotate_every_two(t, even_lane):
    n = t.shape[-1]
    nxt = pltpu.roll(t, n - 1, 1)
    prv = pltpu.roll(t, 1, 1)
    return jnp.where(even_lane, -nxt, prv)


def _proj_kernel(x_ref, gain_ref, w_ref, cos_ref, sin_ref, poolw_ref, poolscale_ref,
                 rq_ref, rk_ref, rv_ref, sg_ref, ypool_ref, mqt_ref, mk_ref, mvt_ref,
                 halo_ref):
    tm = x_ref.shape[1]
    st = pl.program_id(1)
    h = _rms(x_ref[0], gain_ref[...]).astype(BF16)

    def cols(lo, width):
        return _dot(h, w_ref[:, lo:lo + width])

    cos = cos_ref[...]
    sin = sin_ref[...]
    even_lane = (lax.broadcasted_iota(jnp.int32, (tm, RET_QK), 1) % 2) == 0
    q = cols(0, RET_QK)
    rq_ref[0] = (q * cos + _rotate_every_two(q, even_lane) * sin).astype(BF16)
    k = cols(RET_QK, RET_QK)
    k = (k * cos + _rotate_every_two(k, even_lane) * sin) * (RET_QK_DIM ** -0.5)
    rk_ref[0] = k.astype(BF16)
    rv_ref[0] = cols(2 * RET_QK, RET_WIDTH).astype(BF16)
    g = cols(2 * RET_QK + RET_WIDTH, RET_WIDTH)
    sg_ref[0] = (g * jax.nn.sigmoid(g)).astype(BF16)

    off = 2 * RET_QK + 2 * RET_WIDTH
    u = cols(off, POOL_WIDTH)

    first_tile = st == 0
    halo = jnp.where(first_tile, 0.0, halo_ref[...])
    ext = jnp.concatenate([halo, u], axis=0)
    halo_ref[...] = u[tm - POOL_HALO:, :]
    group = lax.broadcasted_iota(jnp.int32, (tm, POOL_WIDTH), 1) // POOL_GROUP_DIM
    group_row = lax.broadcasted_iota(jnp.int32, (1, POOL_WIDTH), 1) // POOL_GROUP_DIM
    wsum = None
    window = None
    acc = ext
    for gi, win in enumerate(POOL_WINDOWS):
        acc = acc + pltpu.roll(acc, win // 2, 0)
        cur = acc[POOL_HALO:, :]
        wsum = cur if wsum is None else jnp.where(group == gi, cur, wsum)
        wl = jnp.full((1, POOL_WIDTH), float(win), F32)
        window = wl if window is None else jnp.where(group_row == gi, wl, window)

    pos = (lax.broadcasted_iota(jnp.int32, (POOL_HALO, POOL_WIDTH), 0) + 1).astype(F32)
    short = jnp.where(first_tile, jnp.minimum(pos, window), window)
    pooled = jnp.concatenate([wsum[:POOL_HALO] / short, wsum[POOL_HALO:] * (1.0 / window)],
                             axis=0) - u

    off += POOL_WIDTH
    mq = cols(off, MOBA_WIDTH) * (MOBA_HEAD_DIM ** -0.5 * LOG2_E)
    mqt_ref[0] = mq.T.astype(BF16)
    mk_ref[0] = cols(off + MOBA_WIDTH, MOBA_WIDTH).astype(BF16)
    mv = cols(off + 2 * MOBA_WIDTH, MOBA_WIDTH)
    ones = jnp.ones((MOBA_VT_ROWS - MOBA_HEAD_DIM, MOBA_BLOCK), F32)
    for j in range(tm // MOBA_BLOCK):
        mvt = mv[j * MOBA_BLOCK:(j + 1) * MOBA_BLOCK, :].T
        pieces = []
        for hd in range(MOBA_HEADS):
            pieces += [mvt[hd * MOBA_HEAD_DIM:(hd + 1) * MOBA_HEAD_DIM, :], ones]
        mvt_ref[0, j] = jnp.concatenate(pieces, axis=0).astype(BF16)

    y_pool = _dot(pooled.astype(BF16), poolw_ref[...]) * poolscale_ref[...]
    ypool_ref[0] = y_pool.astype(BF16)


def _proj(x, gain, w_in, cos, sin, pool_w, pool_scale, layer):
    b, s, _ = x.shape
    tm = TOKEN_TILE
    in_cols = w_in.shape[-1]
    nb = s // MOBA_BLOCK
    bpt = tm // MOBA_BLOCK
    tok = lambda width: pl.BlockSpec((1, tm, width), lambda bi, si: (bi, si, 0))
    out_shape = [
        jax.ShapeDtypeStruct((b, s, RET_QK), BF16),
        jax.ShapeDtypeStruct((b, s, RET_QK), BF16),
        jax.ShapeDtypeStruct((b, s, RET_WIDTH), BF16),
        jax.ShapeDtypeStruct((b, s, RET_WIDTH), BF16),
        jax.ShapeDtypeStruct((b, s, POOL_WIDTH), BF16),
        jax.ShapeDtypeStruct((b, MOBA_WIDTH, s), BF16),
        jax.ShapeDtypeStruct((b, s, MOBA_WIDTH), BF16),
        jax.ShapeDtypeStruct((b, nb, MOBA_HEADS * MOBA_VT_ROWS, MOBA_BLOCK), BF16),
    ]
    out_specs = [
        tok(RET_QK), tok(RET_QK), tok(RET_WIDTH), tok(RET_WIDTH), tok(POOL_WIDTH),
        pl.BlockSpec((1, MOBA_WIDTH, tm), lambda bi, si: (bi, 0, si)),
        tok(MOBA_WIDTH),
        pl.BlockSpec((1, bpt, MOBA_HEADS * MOBA_VT_ROWS, MOBA_BLOCK), lambda bi, si: (bi, si, 0, 0)),
    ]
    return pl.pallas_call(
        _proj_kernel,
        grid=(b, s // tm),
        in_specs=[
            pl.BlockSpec((1, tm, D_MODEL), lambda bi, si: (bi, si, 0)),
            _resident((None, 1, D_MODEL), lambda bi, si: (layer, 0, 0)),
            _resident((None, D_MODEL, in_cols), lambda bi, si: (layer, 0, 0)),
            pl.BlockSpec((tm, RET_QK), lambda bi, si: (si, 0)),
            pl.BlockSpec((tm, RET_QK), lambda bi, si: (si, 0)),
            _resident((None, POOL_WIDTH, POOL_WIDTH), lambda bi, si: (layer, 0, 0)),
            _resident((None, 1, POOL_WIDTH), lambda bi, si: (layer, 0, 0)),
        ],
        out_specs=out_specs,
        out_shape=out_shape,
        scratch_shapes=[pltpu.VMEM((POOL_HALO, POOL_WIDTH), F32)],
        compiler_params=_params("parallel", "arbitrary"),
        name="proj",
    )(x, gain, w_in, cos, sin, pool_w, pool_scale)


def _retention_kernel(q_ref, k_ref, v_ref, sg_ref, dmask_ref, xi_ref, zeta_ref, decay_ref,
                      ondiag_ref, o_ref, state_ref):
    @pl.when(pl.program_id(1) == 0)
    def _():
        state_ref[...] = jnp.zeros_like(state_ref)

    chunk = RET_CHUNK
    lane_head = lax.broadcasted_iota(jnp.int32, (chunk, RET_QK), 1) // RET_QK_DIM
    state = state_ref[...]
    for ci in range(q_ref.shape[1] // chunk):
        rows = slice(ci * chunk, (ci + 1) * chunk)
        q = q_ref[0, rows, :]
        k = k_ref[0, rows, :]
        v = v_ref[0, rows, :]
        kz = (k.astype(F32) * zeta_ref[...]).astype(BF16)
        update = _dot_tn(kz, v)
        cross = _dot(q, state.astype(BF16)) * xi_ref[...]
        qk = [_dot_nt(jnp.where(lane_head == hd, q, jnp.zeros_like(q)), k)
              for hd in range(RET_HEADS)]
        for hd in range(RET_HEADS):
            vcols = slice(hd * RET_V_DIM, (hd + 1) * RET_V_DIM)
            inner = (qk[hd] * dmask_ref[hd]).astype(BF16)
            o = _dot(inner, v[:, vcols]) + cross[:, vcols]
            mu = jnp.mean(o, axis=-1, keepdims=True)
            var = jnp.mean(jnp.square(o - mu), axis=-1, keepdims=True)
            o = (o - mu) * lax.rsqrt(var + EPS)
            o_ref[0, rows, vcols] = (o * sg_ref[0, rows, vcols].astype(F32)).astype(BF16)
        state = decay_ref[...] * state + ondiag_ref[...] * update
    state_ref[...] = state


def _retention(rq, rk, rv, sg, dmask, xi, zeta, decay, on_diag):
    b, s, _ = rq.shape
    c = RET_CHUNK * RET_STEP_CHUNKS
    tok = lambda width: pl.BlockSpec((1, c, width), lambda bi, ci: (bi, ci, 0))
    const = lambda a: _resident(a.shape, lambda bi, ci: (0,) * a.ndim)
    return pl.pallas_call(
        _retention_kernel,
        grid=(b, s // c),
        in_specs=[tok(RET_QK), tok(RET_QK), tok(RET_WIDTH), tok(RET_WIDTH),
                  const(dmask), const(xi), const(zeta), const(decay), const(on_diag)],
        out_specs=tok(RET_WIDTH),
        out_shape=jax.ShapeDtypeStruct((b, s, RET_WIDTH), BF16),
        scratch_shapes=[pltpu.VMEM((RET_QK, RET_WIDTH), F32)],
        compiler_params=_params("parallel", "arbitrary"),
        name="retention",
    )(rq, rk, rv, sg, dmask, xi, zeta, decay, on_diag)


def _retention_tables():
    c = RET_CHUNK
    log_gamma = jnp.log(1.0 - jnp.power(2.0, -5.0 - jnp.arange(RET_HEADS, dtype=F32)))
    idx = jnp.arange(c, dtype=F32)
    diff = idx[:, None] - idx[None, :]
    dmask = jnp.where(diff >= 0, jnp.exp(log_gamma[:, None, None] * jnp.maximum(diff, 0.0)), 0.0)
    xi = jnp.exp(log_gamma[:, None] * (idx + 1.0))
    zeta = jnp.exp(log_gamma[:, None] * (c - 1.0 - idx))
    g_chunk = jnp.exp(log_gamma * c)
    xi_t = jnp.repeat(xi.T, RET_V_DIM, axis=1)
    zeta_t = jnp.repeat(zeta.T, RET_QK_DIM, axis=1)
    head_of_row = np.arange(RET_QK) // RET_QK_DIM
    head_of_col = np.arange(RET_WIDTH) // RET_V_DIM
    on_diag = jnp.asarray(head_of_row[:, None] == head_of_col[None, :], dtype=F32)
    decay = on_diag * jnp.repeat(g_chunk, RET_QK_DIM)[:, None]
    return dmask, xi_t, zeta_t, decay, on_diag


def _rotation_tables(seq):
    pos = jnp.arange(seq, dtype=F32)
    angle = 1.0 / (10000.0 ** jnp.linspace(0.0, 1.0, RET_QK_DIM // 2, dtype=F32))
    angle = jnp.repeat(angle, 2)
    ang = pos[:, None] * angle[None, :]
    return jnp.tile(jnp.cos(ang), (1, RET_HEADS)), jnp.tile(jnp.sin(ang), (1, RET_HEADS))


def _moba_kernel(qt_ref, k_ref, vt_ref, o_ref,
                 kmean_ref, kabs_ref, qh_ref, shift_ref, bound_ref, m_ref, acc_ref, p_ref,
                 s_even_ref, s_odd_ref, top_even_ref, top_odd_ref):
    qi = pl.program_id(1)
    nb = kmean_ref.shape[0]
    blk = MOBA_BLOCK
    tq = qt_ref.shape[2]
    heads = range(MOBA_HEADS)
    vt_rows = [slice(hd * MOBA_VT_ROWS, (hd + 1) * MOBA_VT_ROWS) for hd in heads]
    chunks = [slice(r, r + MOBA_ROW_CHUNK) for r in range(0, blk, MOBA_ROW_CHUNK)]

    @pl.when(qi == 0)
    def _():
        for j in range(nb):
            kb = k_ref[0, j * blk:(j + 1) * blk, :].astype(F32)
            kmean_ref[j:j + 1, :] = jnp.mean(kb, axis=0, keepdims=True)
            kabs_ref[j:j + 1, :] = jnp.max(jnp.abs(kb), axis=0, keepdims=True)

    qt = qt_ref[0]
    row_head = lax.broadcasted_iota(jnp.int32, (MOBA_WIDTH, tq), 0) // MOBA_HEAD_DIM
    for hd in heads:
        qh_ref[hd] = jnp.where(row_head == hd, qt, jnp.zeros_like(qt))

    def key_block(j):
        return k_ref[0, pl.ds(pl.multiple_of(j * blk, blk), blk), :]

    def stage_scores(j, s_ref, heads):
        kj = key_block(j)
        for hd in heads:
            s_ref[hd] = _dot(kj, qh_ref[hd])

    kmean = kmean_ref[...].astype(BF16)
    kabs = (kabs_ref[...] * MOBA_BOUND_SLACK).astype(BF16)
    gates = [_dot(kmean, qh_ref[hd]) for hd in heads]
    score_bounds = [_dot(kabs, jnp.abs(qh_ref[hd])) for hd in heads]
    stage_scores(0, s_even_ref, heads)

    blk_id = lax.broadcasted_iota(jnp.int32, (nb, tq), 0)
    past = blk_id < qi
    for hd in heads:
        gate = jnp.where(past, gates[hd], NEG_INF)
        chosen = jnp.zeros((nb, tq), jnp.bool_)
        for _ in range(MOBA_TOPK):
            best = jnp.max(gate, axis=0, keepdims=True)
            idx = jnp.min(jnp.where(gate == best, blk_id, nb), axis=0, keepdims=True)
            hit = blk_id == idx
            chosen = jnp.logical_or(chosen, jnp.logical_and(hit, past))
            gate = jnp.where(hit, NEG_INF, gate)
        attended = jnp.logical_or(chosen, blk_id == qi)
        bound = jnp.max(jnp.where(attended, score_bounds[hd], NEG_INF), axis=0, keepdims=True)
        bound_ref[hd:hd + 1, :] = bound
        shift_ref[hd] = jnp.where(chosen, bound, float("inf"))

    def causal(s, row0):
        key_pos = row0 + lax.broadcasted_iota(jnp.int32, s.shape, 0)
        qry_pos = lax.broadcasted_iota(jnp.int32, s.shape, 1)
        return jnp.where(key_pos <= qry_pos, s, NEG_INF)

    def finish():
        out = []
        for hd in heads:
            base = hd * MOBA_VT_ROWS
            denom = acc_ref[base + MOBA_HEAD_DIM:base + MOBA_HEAD_DIM + 1, :]
            out.append(acc_ref[base:base + MOBA_HEAD_DIM, :] / denom)
        o_ref[0] = jnp.concatenate(out, axis=0).T.astype(BF16)

    def stage_values(j, s_ref, slot, heads, own=False):
        for hd in heads:
            shift = bound_ref[hd:hd + 1, :] if own else shift_ref[hd, pl.ds(j, 1), :]
            for c in chunks:
                s = s_ref[hd, c, :]
                s = causal(s, c.start) if own else s
                p_ref[slot, hd, c, :] = jnp.exp2(s - shift).astype(BF16)
            pv = _dot(vt_ref[0, j, vt_rows[hd], :], p_ref[slot, hd])
            acc_ref[vt_rows[hd], :] = acc_ref[vt_rows[hd], :] + pv

    acc_ref[...] = jnp.zeros_like(acc_ref)
    head_groups = [(0, 1), (2, 3)]

    def trips(first_block, n_blocks):
        def body(t, carry):
            for i in range(n_blocks):
                j = jnp.minimum(first_block + n_blocks * t + i, qi)
                cur, nxt = (s_even_ref, s_odd_ref) if i % 2 == 0 else (s_odd_ref, s_even_ref)
                for g in head_groups:
                    stage_scores(jnp.minimum(j + 1, qi), nxt, g)
                    stage_values(j, cur, i % 2, g)
            return carry
        return body

    long_trips = qi // MOBA_TRIP_BLOCKS
    covered = long_trips * MOBA_TRIP_BLOCKS
    lax.fori_loop(0, long_trips, trips(0, MOBA_TRIP_BLOCKS), 0)
    lax.fori_loop(0, (qi - covered + 1) // 2, trips(covered, 2), 0)
    stage_values(qi, s_even_ref, 0, heads, own=True)

    sums = [acc_ref[hd * MOBA_VT_ROWS + MOBA_HEAD_DIM:hd * MOBA_VT_ROWS + MOBA_HEAD_DIM + 1, :]
            for hd in heads]
    in_range = jnp.min(jnp.concatenate(sums, axis=0)) >= MOBA_MIN_SUM

    @pl.when(in_range)
    def _():
        finish()

    def col_max(tiles):
        top = tiles[0]
        for t in tiles[1:]:
            top = jnp.maximum(top, t)
        return jnp.max(top, axis=0, keepdims=True)

    def scores(j, buf, heads=heads):
        s_ref, top_ref = buf
        kj = key_block(j)
        for hd in heads:
            s = _dot(kj, qh_ref[hd])
            s_ref[hd] = s
            top_ref[hd:hd + 1, :] = col_max([s[c] for c in chunks])

    def attend(j, buf, own, heads=heads):
        s_ref, top_ref = buf
        for hd in heads:
            def chunk(c):
                s = s_ref[hd, c, :]
                return causal(s, c.start) if own else s

            m_old = m_ref[hd:hd + 1, :]
            if own:
                m_new = jnp.maximum(m_old, col_max([chunk(c) for c in chunks]))
                shift = m_new
            else:
                pen = shift_ref[hd, pl.ds(j, 1), :] - bound_ref[hd:hd + 1, :]
                m_new = jnp.maximum(m_old, top_ref[hd:hd + 1, :] - pen)
                shift = m_new + pen
            m_ref[hd:hd + 1, :] = m_new
            for c in chunks:
                p_ref[0, hd, c, :] = jnp.exp2((chunk(c) - shift).astype(BF16))
            pv = _dot(vt_ref[0, j, vt_rows[hd], :], p_ref[0, hd])
            acc_ref[vt_rows[hd], :] = jnp.exp2(m_old - m_new) * acc_ref[vt_rows[hd], :] + pv

    @pl.when(jnp.logical_not(in_range))
    def _():
        m_ref[...] = jnp.full(m_ref.shape, MOBA_M_INIT, F32)
        acc_ref[...] = jnp.zeros_like(acc_ref)
        even = (s_even_ref, top_even_ref)
        odd = (s_odd_ref, top_odd_ref)
        scores(0, even)

        def body(t, carry):
            j = 2 * t
            scores(j + 1, odd)
            attend(j, even, own=False)
            scores(jnp.minimum(j + 2, qi), even)
            attend(j + 1, odd, own=False)
            return carry

        lax.fori_loop(0, (qi + 1) // 2, body, 0)
        attend(qi, even, own=True)
        finish()


def _moba(mqt, mk, mvt):
    b, s, _ = mk.shape
    nb = s // MOBA_BLOCK
    tq = MOBA_BLOCK
    vt_rows = MOBA_HEADS * MOBA_VT_ROWS
    block_stat_buf = pltpu.VMEM((nb, MOBA_WIDTH), F32)
    scores_buf = pltpu.VMEM((MOBA_HEADS, MOBA_BLOCK, tq), F32)
    head_rows_buf = pltpu.VMEM((8, tq), F32)
    return pl.pallas_call(
        _moba_kernel,
        grid=(b, nb),
        in_specs=[
            pl.BlockSpec((1, MOBA_WIDTH, tq), lambda bi, qi: (bi, 0, qi)),
            pl.BlockSpec((1, s, MOBA_WIDTH), lambda bi, qi: (bi, 0, 0)),
            pl.BlockSpec((1, nb, vt_rows, MOBA_BLOCK), lambda bi, qi: (bi, 0, 0, 0)),
        ],
        out_specs=pl.BlockSpec((1, tq, MOBA_WIDTH), lambda bi, qi: (bi, qi, 0)),
        out_shape=jax.ShapeDtypeStruct((b, s, MOBA_WIDTH), BF16),
        scratch_shapes=[
            block_stat_buf, block_stat_buf,
            pltpu.VMEM((MOBA_HEADS, MOBA_WIDTH, tq), BF16),
            pltpu.VMEM((MOBA_HEADS, nb, tq), F32),
            head_rows_buf,
            head_rows_buf,
            pltpu.VMEM((vt_rows, tq), F32),
            pltpu.VMEM((2, MOBA_HEADS, MOBA_BLOCK, tq), BF16),
            scores_buf, scores_buf,
            head_rows_buf, head_rows_buf,
        ],
        compiler_params=_params("parallel", "arbitrary"),
        name="moba",
    )(mqt, mk, mvt)


def _tail_kernel(x_ref, yr_ref, yp_ref, ym_ref, p_ref, wout_ref,
                 ffn_gain_ref, wg_ref, wu_ref, wd_ref,
                 ple_gain_ref, ple_wgate_ref, ple_wproj_ref, final_gain_ref,
                 o_ref, act_ref, *, final):
    x = x_ref[...]
    x = x + _dot(yr_ref[...], wout_ref[:RET_WIDTH, :])
    x = x + _dot(yp_ref[...], wout_ref[RET_WIDTH:RET_WIDTH + POOL_WIDTH, :])
    x = x + _dot(ym_ref[...], wout_ref[RET_WIDTH + POOL_WIDTH:, :])
    emb = _dot(p_ref[...].astype(BF16), ple_wproj_ref[...])
    x = _swiglu_half_step(x, ffn_gain_ref[...], wg_ref, wu_ref, wd_ref, act_ref)
    h = _rms(x, ple_gain_ref[...]).astype(BF16)
    x = x + jax.nn.sigmoid(_dot(h, ple_wgate_ref[...])) * emb
    if final:
        x = _rms(x, final_gain_ref[...])
    o_ref[...] = x


def _tail(x, y_ret, y_pool, y_moba, p, w_out, ffn, ple_gain, ple_w_gate, ple_w_proj,
          final_gain, layer, final):
    tokens = x.shape[0]
    tm = TOKEN_TILE
    ffn_gain, wg, wu, wd = ffn
    tok = lambda width: pl.BlockSpec((tm, width), lambda i: (i, 0))
    per_layer = lambda *shape: _resident((None,) + shape, lambda i: (layer,) + (0,) * len(shape))
    return pl.pallas_call(
        functools.partial(_tail_kernel, final=final),
        grid=(tokens // tm,),
        in_specs=[
            tok(D_MODEL), tok(RET_WIDTH), tok(POOL_WIDTH), tok(MOBA_WIDTH),
            pl.BlockSpec((None, tm, PLE_DIM), lambda i: (layer, i, 0)),
            per_layer(D_MODEL, D_MODEL),
            per_layer(1, D_MODEL), per_layer(D_MODEL, D_FF), per_layer(D_MODEL, D_FF),
            per_layer(D_FF, D_MODEL),
            per_layer(1, D_MODEL), per_layer(D_MODEL, D_MODEL), per_layer(PLE_DIM, D_MODEL),
            _resident((1, D_MODEL), lambda i: (0, 0)),
        ],
        out_specs=tok(D_MODEL),
        out_shape=jax.ShapeDtypeStruct(x.shape, F32),
        scratch_shapes=[pltpu.VMEM((tm, D_FF), BF16)],
        compiler_params=_params("parallel"),
        name="tail",
    )(x, y_ret, y_pool, y_moba, p, w_out, ffn_gain, wg, wu, wd,
      ple_gain, ple_w_gate, ple_w_proj, final_gain)


def _block_diag(w):
    depth, groups, d, _ = w.shape
    eye = jnp.eye(groups, dtype=w.dtype)
    return jnp.einsum("lgcd,gh->lgchd", w, eye).reshape(depth, groups * d, groups * d)


def kernel(x, p, norm_ffn1, ffn1_w_gate, ffn1_w_up, ffn1_w_down, norm_mix, w_in, pool_w, pool_scale, w_out, norm_ffn2, ffn2_w_gate, ffn2_w_up, ffn2_w_down, norm_ple, ple_w_gate, ple_w_proj, norm_final):
    b, s, d = x.shape
    depth = w_in.shape[0]
    tokens = b * s
    bf = lambda w: w.astype(BF16)
    row = lambda g: g.reshape(g.shape[0], 1, g.shape[1])

    ffn1 = (row(norm_ffn1), bf(ffn1_w_gate), bf(ffn1_w_up), bf(ffn1_w_down))
    ffn2 = (row(norm_ffn2), bf(ffn2_w_gate), bf(ffn2_w_up), bf(ffn2_w_down))
    w_in_b, w_out_b = bf(w_in), bf(w_out)
    pool_w_b = bf(_block_diag(pool_w))
    ple_gate_b, ple_proj_b = bf(ple_w_gate), bf(ple_w_proj)
    norm_mix_r, norm_ple_r, pool_scale_r = row(norm_mix), row(norm_ple), row(pool_scale)
    final_gain = norm_final.reshape(1, d)
    p_flat = p.reshape(depth, tokens, PLE_DIM)
    cos, sin = _rotation_tables(s)
    ret_tables = _retention_tables()

    xf = x.reshape(tokens, d)
    for i in range(depth):
        xf = _ffn(xf, *ffn1, i)
        rq, rk, rv, sg, y_pool, mqt, mk, mvt = _proj(
            xf.reshape(b, s, d), norm_mix_r, w_in_b, cos, sin, pool_w_b, pool_scale_r, i)
        y_ret = _retention(rq, rk, rv, sg, *ret_tables)
        y_moba = _moba(mqt, mk, mvt)
        xf = _tail(xf, y_ret.reshape(tokens, RET_WIDTH), y_pool.reshape(tokens, POOL_WIDTH),
                   y_moba.reshape(tokens, MOBA_WIDTH), p_flat, w_out_b, ffn2,
                   norm_ple_r, ple_gate_b, ple_proj_b, final_gain, i, i == depth - 1)
    return xf.reshape(b, s, d)
```

```python
import functools

import jax
import jax.numpy as jnp
import numpy as np
from jax import lax
from jax.experimental import pallas as pl
from jax.experimental.pallas import tpu as pltpu

D_MODEL = 1024
D_FF = 2816
RET_HEADS = 4
RET_QK_DIM = 64
RET_V_DIM = 128
RET_CHUNK = 256
RET_STEP_CHUNKS = 8
RET_QK = RET_HEADS * RET_QK_DIM
RET_WIDTH = RET_HEADS * RET_V_DIM
POOL_GROUPS = 4
POOL_WINDOWS = (2, 4, 8, 16)
POOL_GROUP_DIM = 64
POOL_WIDTH = POOL_GROUPS * POOL_GROUP_DIM
POOL_HALO = 16
assert POOL_WINDOWS == tuple(2 ** (i + 1) for i in range(POOL_GROUPS)) and POOL_HALO == POOL_WINDOWS[-1]
MOBA_HEADS = 4
MOBA_HEAD_DIM = 64
MOBA_WIDTH = MOBA_HEADS * MOBA_HEAD_DIM
MOBA_BLOCK = 256
MOBA_TOPK = 3
MOBA_VT_ROWS = MOBA_HEAD_DIM + 16
MOBA_ROW_CHUNK = 64
MOBA_M_INIT = -(2.0 ** 100)
MOBA_TRIP_BLOCKS = (8, 4, 2)
assert MOBA_TRIP_BLOCKS[-1] == 2 and all(n % 2 == 0 for n in MOBA_TRIP_BLOCKS)
MOBA_BOUND_SLACK = 1.0 + 2.0 ** -7
MOBA_MIN_SUM = 2.0 ** -100
PLE_DIM = 256
EPS = 1e-6

FF_CHUNK = 256
TOKEN_TILE = 512
VMEM_LIMIT = 56 * 1024 * 1024

BF16 = jnp.bfloat16
F32 = jnp.float32
NEG_INF = float("-inf")
LOG2_E = 1.4426950408889634


def _params(*semantics):
    return pltpu.CompilerParams(dimension_semantics=semantics, vmem_limit_bytes=VMEM_LIMIT)


def _resident(shape, index_map):
    return pl.BlockSpec(shape, index_map, pipeline_mode=pl.Buffered(1))


def _rms(x, gain):
    return x * lax.rsqrt(jnp.mean(x * x, axis=-1, keepdims=True) + EPS) * gain


def _dot(a, b):
    return jnp.dot(a, b, preferred_element_type=F32)


def _dot_nt(a, b):
    return lax.dot_general(a, b, (((1,), (1,)), ((), ())), preferred_element_type=F32)


def _dot_tn(a, b):
    return lax.dot_general(a, b, (((0,), (0,)), ((), ())), preferred_element_type=F32)


def _swiglu_half_step(x, gain, wg_ref, wu_ref, wd_ref, act_ref):
    h = _rms(x, gain).astype(BF16)
    for c in range(D_FF // FF_CHUNK):
        cols = slice(c * FF_CHUNK, (c + 1) * FF_CHUNK)
        gate = _dot(h, wg_ref[:, cols])
        up = _dot(h, wu_ref[:, cols])
        act_ref[:, cols] = (gate * jax.nn.sigmoid(gate) * up).astype(BF16)
    return x + 0.5 * _dot(act_ref[...], wd_ref[...])


def _ffn_kernel(x_ref, gain_ref, wg_ref, wu_ref, wd_ref, o_ref, act_ref):
    o_ref[...] = _swiglu_half_step(x_ref[...], gain_ref[...], wg_ref, wu_ref, wd_ref, act_ref)


def _ffn(x, gain, wg, wu, wd, layer):
    tokens = x.shape[0]
    tm = TOKEN_TILE
    return pl.pallas_call(
        _ffn_kernel,
        grid=(tokens // tm,),
        in_specs=[
            pl.BlockSpec((tm, D_MODEL), lambda i: (i, 0)),
            _resident((None, 1, D_MODEL), lambda i: (layer, 0, 0)),
            _resident((None, D_MODEL, D_FF), lambda i: (layer, 0, 0)),
            _resident((None, D_MODEL, D_FF), lambda i: (layer, 0, 0)),
            _resident((None, D_FF, D_MODEL), lambda i: (layer, 0, 0)),
        ],
        out_specs=pl.BlockSpec((tm, D_MODEL), lambda i: (i, 0)),
        out_shape=jax.ShapeDtypeStruct(x.shape, F32),
        scratch_shapes=[pltpu.VMEM((tm, D_FF), BF16)],
        compiler_params=_params("parallel"),
        name="ffn",
    )(x, gain, wg, wu, wd)


def _rotate_every_two(t, even_lane):
    n = t.shape[-1]
    nxt = pltpu.roll(t, n - 1, 1)
    prv = pltpu.roll(t, 1, 1)
    return jnp.where(even_lane, -nxt, prv)


def _proj_kernel(x_ref, gain_ref, w_ref, cos_ref, sin_ref, poolw_ref, poolscale_ref,
                 rq_ref, rk_ref, rv_ref, sg_ref, ypool_ref, mqt_ref, mk_ref, mvt_ref,
                 halo_ref):
    tm = x_ref.shape[1]
    st = pl.program_id(1)
    h = _rms(x_ref[0], gain_ref[...]).astype(BF16)

    def cols(lo, width):
        return _dot(h, w_ref[:, lo:lo + width])

    cos = cos_ref[...]
    sin = sin_ref[...]
    even_lane = (lax.broadcasted_iota(jnp.int32, (tm, RET_QK), 1) % 2) == 0
    q = cols(0, RET_QK)
    rq_ref[0] = (q * cos + _rotate_every_two(q, even_lane) * sin).astype(BF16)
    k = cols(RET_QK, RET_QK)
    k = (k * cos + _rotate_every_two(k, even_lane) * sin) * (RET_QK_DIM ** -0.5)
    rk_ref[0] = k.astype(BF16)
    rv_ref[0] = cols(2 * RET_QK, RET_WIDTH).astype(BF16)
    g = cols(2 * RET_QK + RET_WIDTH, RET_WIDTH)
    sg_ref[0] = (g * jax.nn.sigmoid(g)).astype(BF16)

    off = 2 * RET_QK + 2 * RET_WIDTH
    u = cols(off, POOL_WIDTH)

    first_tile = st == 0
    halo = jnp.where(first_tile, 0.0, halo_ref[...])
    ext = jnp.concatenate([halo, u], axis=0)
    halo_ref[...] = u[tm - POOL_HALO:, :]
    group = lax.broadcasted_iota(jnp.int32, (tm, POOL_WIDTH), 1) // POOL_GROUP_DIM
    group_row = lax.broadcasted_iota(jnp.int32, (1, POOL_WIDTH), 1) // POOL_GROUP_DIM
    wsum = None
    window = None
    acc = ext
    for gi, win in enumerate(POOL_WINDOWS):
        acc = acc + pltpu.roll(acc, win // 2, 0)
        cur = acc[POOL_HALO:, :]
        wsum = cur if wsum is None else jnp.where(group == gi, cur, wsum)
        wl = jnp.full((1, POOL_WIDTH), float(win), F32)
        window = wl if window is None else jnp.where(group_row == gi, wl, window)

    pos = (lax.broadcasted_iota(jnp.int32, (POOL_HALO, POOL_WIDTH), 0) + 1).astype(F32)
    short = jnp.where(first_tile, jnp.minimum(pos, window), window)
    pooled = jnp.concatenate([wsum[:POOL_HALO] / short, wsum[POOL_HALO:] * (1.0 / window)],
                             axis=0) - u

    off += POOL_WIDTH
    mq = cols(off, MOBA_WIDTH) * (MOBA_HEAD_DIM ** -0.5 * LOG2_E)
    mqt_ref[0] = mq.T.astype(BF16)
    mk_ref[0] = cols(off + MOBA_WIDTH, MOBA_WIDTH).astype(BF16)
    mv = cols(off + 2 * MOBA_WIDTH, MOBA_WIDTH)
    ones = jnp.ones((MOBA_VT_ROWS - MOBA_HEAD_DIM, MOBA_BLOCK), F32)
    for j in range(tm // MOBA_BLOCK):
        mvt = mv[j * MOBA_BLOCK:(j + 1) * MOBA_BLOCK, :].T
        pieces = []
        for hd in range(MOBA_HEADS):
            pieces += [mvt[hd * MOBA_HEAD_DIM:(hd + 1) * MOBA_HEAD_DIM, :], ones]
        mvt_ref[0, j] = jnp.concatenate(pieces, axis=0).astype(BF16)

    y_pool = _dot(pooled.astype(BF16), poolw_ref[...]) * poolscale_ref[...]
    ypool_ref[0] = y_pool.astype(BF16)


def _proj(x, gain, w_in, cos, sin, pool_w, pool_scale, layer):
    b, s, _ = x.shape
    tm = TOKEN_TILE
    in_cols = w_in.shape[-1]
    nb = s // MOBA_BLOCK
    bpt = tm // MOBA_BLOCK
    tok = lambda width: pl.BlockSpec((1, tm, width), lambda bi, si: (bi, si, 0))
    out_shape = [
        jax.ShapeDtypeStruct((b, s, RET_QK), BF16),
        jax.ShapeDtypeStruct((b, s, RET_QK), BF16),
        jax.ShapeDtypeStruct((b, s, RET_WIDTH), BF16),
        jax.ShapeDtypeStruct((b, s, RET_WIDTH), BF16),
        jax.ShapeDtypeStruct((b, s, POOL_WIDTH), BF16),
        jax.ShapeDtypeStruct((b, MOBA_WIDTH, s), BF16),
        jax.ShapeDtypeStruct((b, s, MOBA_WIDTH), BF16),
        jax.ShapeDtypeStruct((b, nb, MOBA_HEADS * MOBA_VT_ROWS, MOBA_BLOCK), BF16),
    ]
    out_specs = [
        tok(RET_QK), tok(RET_QK), tok(RET_WIDTH), tok(RET_WIDTH), tok(POOL_WIDTH),
        pl.BlockSpec((1, MOBA_WIDTH, tm), lambda bi, si: (bi, 0, si)),
        tok(MOBA_WIDTH),
        pl.BlockSpec((1, bpt, MOBA_HEADS * MOBA_VT_ROWS, MOBA_BLOCK), lambda bi, si: (bi, si, 0, 0)),
    ]
    return pl.pallas_call(
        _proj_kernel,
        grid=(b, s // tm),
        in_specs=[
            pl.BlockSpec((1, tm, D_MODEL), lambda bi, si: (bi, si, 0)),
            _resident((None, 1, D_MODEL), lambda bi, si: (layer, 0, 0)),
            _resident((None, D_MODEL, in_cols), lambda bi, si: (layer, 0, 0)),
            pl.BlockSpec((tm, RET_QK), lambda bi, si: (si, 0)),
            pl.BlockSpec((tm, RET_QK), lambda bi, si: (si, 0)),
            _resident((None, POOL_WIDTH, POOL_WIDTH), lambda bi, si: (layer, 0, 0)),
            _resident((None, 1, POOL_WIDTH), lambda bi, si: (layer, 0, 0)),
        ],
        out_specs=out_specs,
        out_shape=out_shape,
        scratch_shapes=[pltpu.VMEM((POOL_HALO, POOL_WIDTH), F32)],
        compiler_params=_params("parallel", "arbitrary"),
        name="proj",
    )(x, gain, w_in, cos, sin, pool_w, pool_scale)


def _retention_kernel(q_ref, k_ref, v_ref, sg_ref, dmask_ref, xi_ref, zeta_ref, decay_ref,
                      ondiag_ref, o_ref, state_ref):
    @pl.when(pl.program_id(1) == 0)
    def _():
        state_ref[...] = jnp.zeros_like(state_ref)

    chunk = RET_CHUNK
    lane_head = lax.broadcasted_iota(jnp.int32, (chunk, RET_QK), 1) // RET_QK_DIM
    state = state_ref[...]
    for ci in range(q_ref.shape[1] // chunk):
        rows = slice(ci * chunk, (ci + 1) * chunk)
        q = q_ref[0, rows, :]
        k = k_ref[0, rows, :]
        v = v_ref[0, rows, :]
        kz = (k.astype(F32) * zeta_ref[...]).astype(BF16)
        update = _dot_tn(kz, v)
        cross = _dot(q, state.astype(BF16)) * xi_ref[...]
        qk = [_dot_nt(jnp.where(lane_head == hd, q, jnp.zeros_like(q)), k)
              for hd in range(RET_HEADS)]
        for hd in range(RET_HEADS):
            vcols = slice(hd * RET_V_DIM, (hd + 1) * RET_V_DIM)
            inner = (qk[hd] * dmask_ref[hd]).astype(BF16)
            o = _dot(inner, v[:, vcols]) + cross[:, vcols]
            mu = jnp.mean(o, axis=-1, keepdims=True)
            var = jnp.mean(jnp.square(o - mu), axis=-1, keepdims=True)
            o = (o - mu) * lax.rsqrt(var + EPS)
            o_ref[0, rows, vcols] = (o * sg_ref[0, rows, vcols].astype(F32)).astype(BF16)
        state = decay_ref[...] * state + ondiag_ref[...] * update
    state_ref[...] = state


def _retention(rq, rk, rv, sg, dmask, xi, zeta, decay, on_diag):
    b, s, _ = rq.shape
    c = RET_CHUNK * RET_STEP_CHUNKS
    tok = lambda width: pl.BlockSpec((1, c, width), lambda bi, ci: (bi, ci, 0))
    const = lambda a: _resident(a.shape, lambda bi, ci: (0,) * a.ndim)
    return pl.pallas_call(
        _retention_kernel,
        grid=(b, s // c),
        in_specs=[tok(RET_QK), tok(RET_QK), tok(RET_WIDTH), tok(RET_WIDTH),
                  const(dmask), const(xi), const(zeta), const(decay), const(on_diag)],
        out_specs=tok(RET_WIDTH),
        out_shape=jax.ShapeDtypeStruct((b, s, RET_WIDTH), BF16),
        scratch_shapes=[pltpu.VMEM((RET_QK, RET_WIDTH), F32)],
        compiler_params=_params("parallel", "arbitrary"),
        name="retention",
    )(rq, rk, rv, sg, dmask, xi, zeta, decay, on_diag)


def _retention_tables():
    c = RET_CHUNK
    log_gamma = jnp.log(1.0 - jnp.power(2.0, -5.0 - jnp.arange(RET_HEADS, dtype=F32)))
    idx = jnp.arange(c, dtype=F32)
    diff = idx[:, None] - idx[None, :]
    dmask = jnp.where(diff >= 0, jnp.exp(log_gamma[:, None, None] * jnp.maximum(diff, 0.0)), 0.0)
    xi = jnp.exp(log_gamma[:, None] * (idx + 1.0))
    zeta = jnp.exp(log_gamma[:, None] * (c - 1.0 - idx))
    g_chunk = jnp.exp(log_gamma * c)
    xi_t = jnp.repeat(xi.T, RET_V_DIM, axis=1)
    zeta_t = jnp.repeat(zeta.T, RET_QK_DIM, axis=1)
    head_of_row = np.arange(RET_QK) // RET_QK_DIM
    head_of_col = np.arange(RET_WIDTH) // RET_V_DIM
    on_diag = jnp.asarray(head_of_row[:, None] == head_of_col[None, :], dtype=F32)
    decay = on_diag * jnp.repeat(g_chunk, RET_QK_DIM)[:, None]
    return dmask, xi_t, zeta_t, decay, on_diag


def _rotation_tables(seq):
    pos = jnp.arange(seq, dtype=F32)
    angle = 1.0 / (10000.0 ** jnp.linspace(0.0, 1.0, RET_QK_DIM // 2, dtype=F32))
    angle = jnp.repeat(angle, 2)
    ang = pos[:, None] * angle[None, :]
    return jnp.tile(jnp.cos(ang), (1, RET_HEADS)), jnp.tile(jnp.sin(ang), (1, RET_HEADS))


def _moba_kernel(qt_ref, k_ref, vt_ref, o_ref,
                 kmean_ref, kabs_ref, qh_ref, shift_ref, bound_ref, m_ref, acc_ref, p_ref,
                 s_even_ref, s_odd_ref, top_even_ref, top_odd_ref):
    qi = pl.program_id(1)
    nb = kmean_ref.shape[0]
    blk = MOBA_BLOCK
    tq = qt_ref.shape[2]
    heads = range(MOBA_HEADS)
    vt_rows = [slice(hd * MOBA_VT_ROWS, (hd + 1) * MOBA_VT_ROWS) for hd in heads]
    chunks = [slice(r, r + MOBA_ROW_CHUNK) for r in range(0, blk, MOBA_ROW_CHUNK)]

    @pl.when(qi == 0)
    def _():
        for j in range(nb):
            kb = k_ref[0, j * blk:(j + 1) * blk, :].astype(F32)
            kmean_ref[j:j + 1, :] = jnp.mean(kb, axis=0, keepdims=True)
            kabs_ref[j:j + 1, :] = jnp.max(jnp.abs(kb), axis=0, keepdims=True)

    qt = qt_ref[0]
    row_head = lax.broadcasted_iota(jnp.int32, (MOBA_WIDTH, tq), 0) // MOBA_HEAD_DIM
    for hd in heads:
        qh_ref[hd] = jnp.where(row_head == hd, qt, jnp.zeros_like(qt))

    def key_block(j):
        return k_ref[0, pl.ds(pl.multiple_of(j * blk, blk), blk), :]

    def stage_scores(j, s_ref, heads):
        kj = key_block(j)
        for hd in heads:
            s_ref[hd] = _dot(kj, qh_ref[hd])

    kmean = kmean_ref[...].astype(BF16)
    kabs = (kabs_ref[...] * MOBA_BOUND_SLACK).astype(BF16)
    gates = [_dot(kmean, qh_ref[hd]) for hd in heads]
    score_bounds = [_dot(kabs, jnp.abs(qh_ref[hd])) for hd in heads]
    stage_scores(0, s_even_ref, heads)

    blk_id = lax.broadcasted_iota(jnp.int32, (nb, tq), 0)
    past = blk_id < qi
    for hd in heads:
        gate = jnp.where(past, gates[hd], NEG_INF)
        chosen = jnp.zeros((nb, tq), jnp.bool_)
        for _ in range(MOBA_TOPK):
            best = jnp.max(gate, axis=0, keepdims=True)
            idx = jnp.min(jnp.where(gate == best, blk_id, nb), axis=0, keepdims=True)
            hit = blk_id == idx
            chosen = jnp.logical_or(chosen, jnp.logical_and(hit, past))
            gate = jnp.where(hit, NEG_INF, gate)
        attended = jnp.logical_or(chosen, blk_id == qi)
        bound = jnp.max(jnp.where(attended, score_bounds[hd], NEG_INF), axis=0, keepdims=True)
        bound_ref[hd:hd + 1, :] = bound
        shift_ref[hd] = jnp.where(chosen, bound, float("inf"))

    def causal(s, row0):
        key_pos = row0 + lax.broadcasted_iota(jnp.int32, s.shape, 0)
        qry_pos = lax.broadcasted_iota(jnp.int32, s.shape, 1)
        return jnp.where(key_pos <= qry_pos, s, NEG_INF)

    def finish():
        out = []
        for hd in heads:
            base = hd * MOBA_VT_ROWS
            denom = acc_ref[base + MOBA_HEAD_DIM:base + MOBA_HEAD_DIM + 1, :]
            out.append(acc_ref[base:base + MOBA_HEAD_DIM, :] / denom)
        o_ref[0] = jnp.concatenate(out, axis=0).T.astype(BF16)

    def stage_values(j, s_ref, slot, heads, own=False):
        for hd in heads:
            shift = bound_ref[hd:hd + 1, :] if own else shift_ref[hd, pl.ds(j, 1), :]
            for c in chunks:
                s = s_ref[hd, c, :]
                s = causal(s, c.start) if own else s
                p_ref[slot, hd, c, :] = jnp.exp2(s - shift).astype(BF16)
            pv = _dot(vt_ref[0, j, vt_rows[hd], :], p_ref[slot, hd])
            acc_ref[vt_rows[hd], :] = acc_ref[vt_rows[hd], :] + pv

    acc_ref[...] = jnp.zeros_like(acc_ref)
    head_groups = [(0, 1), (2, 3)]

    def trips(first_block, n_blocks):
        def body(t, carry):
            for i in range(n_blocks):
                j = jnp.minimum(first_block + n_blocks * t + i, qi)
                cur, nxt = (s_even_ref, s_odd_ref) if i % 2 == 0 else (s_odd_ref, s_even_ref)
                for g in head_groups:
                    stage_scores(jnp.minimum(j + 1, qi), nxt, g)
                    stage_values(j, cur, i % 2, g)
            return carry
        return body

    covered = 0
    for n_blocks in MOBA_TRIP_BLOCKS[:-1]:
        n_trips = (qi - covered) // n_blocks
        lax.fori_loop(0, n_trips, trips(covered, n_blocks), 0)
        covered = covered + n_trips * n_blocks
    lax.fori_loop(0, (qi - covered + 1) // 2, trips(covered, MOBA_TRIP_BLOCKS[-1]), 0)
    stage_values(qi, s_even_ref, 0, heads, own=True)

    sums = [acc_ref[hd * MOBA_VT_ROWS + MOBA_HEAD_DIM:hd * MOBA_VT_ROWS + MOBA_HEAD_DIM + 1, :]
            for hd in heads]
    in_range = jnp.min(jnp.concatenate(sums, axis=0)) >= MOBA_MIN_SUM

    @pl.when(in_range)
    def _():
        finish()

    def col_max(tiles):
        top = tiles[0]
        for t in tiles[1:]:
            top = jnp.maximum(top, t)
        return jnp.max(top, axis=0, keepdims=True)

    def scores(j, buf, heads=heads):
        s_ref, top_ref = buf
        kj = key_block(j)
        for hd in heads:
            s = _dot(kj, qh_ref[hd])
            s_ref[hd] = s
            top_ref[hd:hd + 1, :] = col_max([s[c] for c in chunks])

    def attend(j, buf, own, heads=heads):
        s_ref, top_ref = buf
        for hd in heads:
            def chunk(c):
                s = s_ref[hd, c, :]
                return causal(s, c.start) if own else s

            m_old = m_ref[hd:hd + 1, :]
            if own:
                m_new = jnp.maximum(m_old, col_max([chunk(c) for c in chunks]))
                shift = m_new
            else:
                pen = shift_ref[hd, pl.ds(j, 1), :] - bound_ref[hd:hd + 1, :]
                m_new = jnp.maximum(m_old, top_ref[hd:hd + 1, :] - pen)
                shift = m_new + pen
            m_ref[hd:hd + 1, :] = m_new
            for c in chunks:
                p_ref[0, hd, c, :] = jnp.exp2((chunk(c) - shift).astype(BF16))
            pv = _dot(vt_ref[0, j, vt_rows[hd], :], p_ref[0, hd])
            acc_ref[vt_rows[hd], :] = jnp.exp2(m_old - m_new) * acc_ref[vt_rows[hd], :] + pv

    @pl.when(jnp.logical_not(in_range))
    def _():
        m_ref[...] = jnp.full(m_ref.shape, MOBA_M_INIT, F32)
        acc_ref[...] = jnp.zeros_like(acc_ref)
        even = (s_even_ref, top_even_ref)
        odd = (s_odd_ref, top_odd_ref)
        scores(0, even)

        def body(t, carry):
            j = 2 * t
            scores(j + 1, odd)
            attend(j, even, own=False)
            scores(jnp.minimum(j + 2, qi), even)
            attend(j + 1, odd, own=False)
            return carry

        lax.fori_loop(0, (qi + 1) // 2, body, 0)
        attend(qi, even, own=True)
        finish()


def _moba(mqt, mk, mvt):
    b, s, _ = mk.shape
    nb = s // MOBA_BLOCK
    tq = MOBA_BLOCK
    vt_rows = MOBA_HEADS * MOBA_VT_ROWS
    block_stat_buf = pltpu.VMEM((nb, MOBA_WIDTH), F32)
    scores_buf = pltpu.VMEM((MOBA_HEADS, MOBA_BLOCK, tq), F32)
    head_rows_buf = pltpu.VMEM((8, tq), F32)
    return pl.pallas_call(
        _moba_kernel,
        grid=(b, nb),
        in_specs=[
            pl.BlockSpec((1, MOBA_WIDTH, tq), lambda bi, qi: (bi, 0, qi)),
            pl.BlockSpec((1, s, MOBA_WIDTH), lambda bi, qi: (bi, 0, 0)),
            pl.BlockSpec((1, nb, vt_rows, MOBA_BLOCK), lambda bi, qi: (bi, 0, 0, 0)),
        ],
        out_specs=pl.BlockSpec((1, tq, MOBA_WIDTH), lambda bi, qi: (bi, qi, 0)),
        out_shape=jax.ShapeDtypeStruct((b, s, MOBA_WIDTH), BF16),
        scratch_shapes=[
            block_stat_buf, block_stat_buf,
            pltpu.VMEM((MOBA_HEADS, MOBA_WIDTH, tq), BF16),
            pltpu.VMEM((MOBA_HEADS, nb, tq), F32),
            head_rows_buf,
            head_rows_buf,
            pltpu.VMEM((vt_rows, tq), F32),
            pltpu.VMEM((2, MOBA_HEADS, MOBA_BLOCK, tq), BF16),
            scores_buf, scores_buf,
            head_rows_buf, head_rows_buf,
        ],
        compiler_params=_params("parallel", "arbitrary"),
        name="moba",
    )(mqt, mk, mvt)


def _tail_kernel(x_ref, yr_ref, yp_ref, ym_ref, p_ref, wout_ref,
                 ffn_gain_ref, wg_ref, wu_ref, wd_ref,
                 ple_gain_ref, ple_wgate_ref, ple_wproj_ref, final_gain_ref,
                 o_ref, act_ref, *, final):
    x = x_ref[...]
    x = x + _dot(yr_ref[...], wout_ref[:RET_WIDTH, :])
    x = x + _dot(yp_ref[...], wout_ref[RET_WIDTH:RET_WIDTH + POOL_WIDTH, :])
    x = x + _dot(ym_ref[...], wout_ref[RET_WIDTH + POOL_WIDTH:, :])
    emb = _dot(p_ref[...].astype(BF16), ple_wproj_ref[...])
    x = _swiglu_half_step(x, ffn_gain_ref[...], wg_ref, wu_ref, wd_ref, act_ref)
    h = _rms(x, ple_gain_ref[...]).astype(BF16)
    x = x + jax.nn.sigmoid(_dot(h, ple_wgate_ref[...])) * emb
    if final:
        x = _rms(x, final_gain_ref[...])
    o_ref[...] = x


def _tail(x, y_ret, y_pool, y_moba, p, w_out, ffn, ple_gain, ple_w_gate, ple_w_proj,
          final_gain, layer, final):
    tokens = x.shape[0]
    tm = TOKEN_TILE
    ffn_gain, wg, wu, wd = ffn
    tok = lambda width: pl.BlockSpec((tm, width), lambda i: (i, 0))
    per_layer = lambda *shape: _resident((None,) + shape, lambda i: (layer,) + (0,) * len(shape))
    return pl.pallas_call(
        functools.partial(_tail_kernel, final=final),
        grid=(tokens // tm,),
        in_specs=[
            tok(D_MODEL), tok(RET_WIDTH), tok(POOL_WIDTH), tok(MOBA_WIDTH),
            pl.BlockSpec((None, tm, PLE_DIM), lambda i: (layer, i, 0)),
            per_layer(D_MODEL, D_MODEL),
            per_layer(1, D_MODEL), per_layer(D_MODEL, D_FF), per_layer(D_MODEL, D_FF),
            per_layer(D_FF, D_MODEL),
            per_layer(1, D_MODEL), per_layer(D_MODEL, D_MODEL), per_layer(PLE_DIM, D_MODEL),
            _resident((1, D_MODEL), lambda i: (0, 0)),
        ],
        out_specs=tok(D_MODEL),
        out_shape=jax.ShapeDtypeStruct(x.shape, F32),
        scratch_shapes=[pltpu.VMEM((tm, D_FF), BF16)],
        compiler_params=_params("parallel"),
        name="tail",
    )(x, y_ret, y_pool, y_moba, p, w_out, ffn_gain, wg, wu, wd,
      ple_gain, ple_w_gate, ple_w_proj, final_gain)


def _block_diag(w):
    depth, groups, d, _ = w.shape
    eye = jnp.eye(groups, dtype=w.dtype)
    return jnp.einsum("lgcd,gh->lgchd", w, eye).reshape(depth, groups * d, groups * d)


def kernel(x, p, norm_ffn1, ffn1_w_gate, ffn1_w_up, ffn1_w_down, norm_mix, w_in, pool_w, pool_scale, w_out, norm_ffn2, ffn2_w_gate, ffn2_w_up, ffn2_w_down, norm_ple, ple_w_gate, ple_w_proj, norm_final):
    b, s, d = x.shape
    depth = w_in.shape[0]
    tokens = b * s
    bf = lambda w: w.astype(BF16)
    row = lambda g: g.reshape(g.shape[0], 1, g.shape[1])

    ffn1 = (row(norm_ffn1), bf(ffn1_w_gate), bf(ffn1_w_up), bf(ffn1_w_down))
    ffn2 = (row(norm_ffn2), bf(ffn2_w_gate), bf(ffn2_w_up), bf(ffn2_w_down))
    w_in_b, w_out_b = bf(w_in), bf(w_out)
    pool_w_b = bf(_block_diag(pool_w))
    ple_gate_b, ple_proj_b = bf(ple_w_gate), bf(ple_w_proj)
    norm_mix_r, norm_ple_r, pool_scale_r = row(norm_mix), row(norm_ple), row(pool_scale)
    final_gain = norm_final.reshape(1, d)
    p_flat = p.reshape(depth, tokens, PLE_DIM)
    cos, sin = _rotation_tables(s)
    ret_tables = _retention_tables()

    xf = x.reshape(tokens, d)
    for i in range(depth):
        xf = _ffn(xf, *ffn1, i)
        rq, rk, rv, sg, y_pool, mqt, mk, mvt = _proj(
            xf.reshape(b, s, d), norm_mix_r, w_in_b, cos, sin, pool_w_b, pool_scale_r, i)
        y_ret = _retention(rq, rk, rv, sg, *ret_tables)
        y_moba = _moba(mqt, mk, mvt)
        xf = _tail(xf, y_ret.reshape(tokens, RET_WIDTH), y_pool.reshape(tokens, POOL_WIDTH),
                   y_moba.reshape(tokens, MOBA_WIDTH), p_flat, w_out_b, ffn2,
                   norm_ple_r, ple_gate_b, ple_proj_b, final_gain, i, i == depth - 1)
    return xf.reshape(b, s, d)
```

```python
import functools

import jax
import jax.numpy as jnp
import numpy as np
from jax import lax
from jax.experimental import pallas as pl
from jax.experimental.pallas import tpu as pltpu

D_MODEL = 1024
D_FF = 2816
RET_HEADS = 4
RET_QK_DIM = 64
RET_V_DIM = 128
RET_CHUNK = 256
RET_STEP_CHUNKS = 8
RET_QK = RET_HEADS * RET_QK_DIM
RET_WIDTH = RET_HEADS * RET_V_DIM
POOL_GROUPS = 4
POOL_WINDOWS = (2, 4, 8, 16)
POOL_GROUP_DIM = 64
POOL_WIDTH = POOL_GROUPS * POOL_GROUP_DIM
POOL_HALO = 16
assert POOL_WINDOWS == tuple(2 ** (i + 1) for i in range(POOL_GROUPS)) and POOL_HALO == POOL_WINDOWS[-1]
MOBA_HEADS = 4
MOBA_HEAD_DIM = 64
MOBA_WIDTH = MOBA_HEADS * MOBA_HEAD_DIM
MOBA_BLOCK = 256
MOBA_TOPK = 3
MOBA_VT_ROWS = MOBA_HEAD_DIM + 16
MOBA_ROW_CHUNK = 64
MOBA_M_INIT = -(2.0 ** 100)
MOBA_TRIP_BLOCKS = (16, 8, 4, 2)
assert MOBA_TRIP_BLOCKS[-1] == 2 and all(n % 2 == 0 for n in MOBA_TRIP_BLOCKS)
MOBA_BOUND_SLACK = 1.0 + 2.0 ** -7
MOBA_MIN_SUM = 2.0 ** -100
PLE_DIM = 256
EPS = 1e-6

FF_CHUNK = 256
TOKEN_TILE = 512
VMEM_LIMIT = 56 * 1024 * 1024

BF16 = jnp.bfloat16
F32 = jnp.float32
NEG_INF = float("-inf")
LOG2_E = 1.4426950408889634


def _params(*semantics):
    return pltpu.CompilerParams(dimension_semantics=semantics, vmem_limit_bytes=VMEM_LIMIT)


def _resident(shape, index_map):
    return pl.BlockSpec(shape, index_map, pipeline_mode=pl.Buffered(1))


def _rms(x, gain):
    return x * lax.rsqrt(jnp.mean(x * x, axis=-1, keepdims=True) + EPS) * gain


def _dot(a, b):
    return jnp.dot(a, b, preferred_element_type=F32)


def _dot_nt(a, b):
    return lax.dot_general(a, b, (((1,), (1,)), ((), ())), preferred_element_type=F32)


def _dot_tn(a, b):
    return lax.dot_general(a, b, (((0,), (0,)), ((), ())), preferred_element_type=F32)


def _swiglu_half_step(x, gain, wg_ref, wu_ref, wd_ref, act_ref):
    h = _rms(x, gain).astype(BF16)
    for c in range(D_FF // FF_CHUNK):
        cols = slice(c * FF_CHUNK, (c + 1) * FF_CHUNK)
        gate = _dot(h, wg_ref[:, cols])
        up = _dot(h, wu_ref[:, cols])
        act_ref[:, cols] = (gate * jax.nn.sigmoid(gate) * up).astype(BF16)
    return x + 0.5 * _dot(act_ref[...], wd_ref[...])


def _ffn_kernel(x_ref, gain_ref, wg_ref, wu_ref, wd_ref, o_ref, act_ref):
    o_ref[...] = _swiglu_half_step(x_ref[...], gain_ref[...], wg_ref, wu_ref, wd_ref, act_ref)


def _ffn(x, gain, wg, wu, wd, layer):
    tokens = x.shape[0]
    tm = TOKEN_TILE
    return pl.pallas_call(
        _ffn_kernel,
        grid=(tokens // tm,),
        in_specs=[
            pl.BlockSpec((tm, D_MODEL), lambda i: (i, 0)),
            _resident((None, 1, D_MODEL), lambda i: (layer, 0, 0)),
            _resident((None, D_MODEL, D_FF), lambda i: (layer, 0, 0)),
            _resident((None, D_MODEL, D_FF), lambda i: (layer, 0, 0)),
            _resident((None, D_FF, D_MODEL), lambda i: (layer, 0, 0)),
        ],
        out_specs=pl.BlockSpec((tm, D_MODEL), lambda i: (i, 0)),
        out_shape=jax.ShapeDtypeStruct(x.shape, F32),
        scratch_shapes=[pltpu.VMEM((tm, D_FF), BF16)],
        compiler_params=_params("parallel"),
        name="ffn",
    )(x, gain, wg, wu, wd)


def _rotate_every_two(t, even_lane):
    n = t.shape[-1]
    nxt = pltpu.roll(t, n - 1, 1)
    prv = pltpu.roll(t, 1, 1)
    return jnp.where(even_lane, -nxt, prv)


def _proj_kernel(x_ref, gain_ref, w_ref, cos_ref, sin_ref, poolw_ref, poolscale_ref,
                 rq_ref, rk_ref, rv_ref, sg_ref, ypool_ref, mqt_ref, mk_ref, mvt_ref,
                 halo_ref):
    tm = x_ref.shape[1]
    st = pl.program_id(1)
    h = _rms(x_ref[0], gain_ref[...]).astype(BF16)

    def cols(lo, width):
        return _dot(h, w_ref[:, lo:lo + width])

    cos = cos_ref[...]
    sin = sin_ref[...]
    even_lane = (lax.broadcasted_iota(jnp.int32, (tm, RET_QK), 1) % 2) == 0
    q = cols(0, RET_QK)
    rq_ref[0] = (q * cos + _rotate_every_two(q, even_lane) * sin).astype(BF16)
    k = cols(RET_QK, RET_QK)
    k = (k * cos + _rotate_every_two(k, even_lane) * sin) * (RET_QK_DIM ** -0.5)
    rk_ref[0] = k.astype(BF16)
    rv_ref[0] = cols(2 * RET_QK, RET_WIDTH).astype(BF16)
    g = cols(2 * RET_QK + RET_WIDTH, RET_WIDTH)
    sg_ref[0] = (g * jax.nn.sigmoid(g)).astype(BF16)

    off = 2 * RET_QK + 2 * RET_WIDTH
    u = cols(off, POOL_WIDTH)

    first_tile = st == 0
    halo = jnp.where(first_tile, 0.0, halo_ref[...])
    ext = jnp.concatenate([halo, u], axis=0)
    halo_ref[...] = u[tm - POOL_HALO:, :]
    group = lax.broadcasted_iota(jnp.int32, (tm, POOL_WIDTH), 1) // POOL_GROUP_DIM
    group_row = lax.broadcasted_iota(jnp.int32, (1, POOL_WIDTH), 1) // POOL_GROUP_DIM
    wsum = None
    window = None
    acc = ext
    for gi, win in enumerate(POOL_WINDOWS):
        acc = acc + pltpu.roll(acc, win // 2, 0)
        cur = acc[POOL_HALO:, :]
        wsum = cur if wsum is None else jnp.where(group == gi, cur, wsum)
        wl = jnp.full((1, POOL_WIDTH), float(win), F32)
        window = wl if window is None else jnp.where(group_row == gi, wl, window)

    pos = (lax.broadcasted_iota(jnp.int32, (POOL_HALO, POOL_WIDTH), 0) + 1).astype(F32)
    short = jnp.where(first_tile, jnp.minimum(pos, window), window)
    pooled = jnp.concatenate([wsum[:POOL_HALO] / short, wsum[POOL_HALO:] * (1.0 / window)],
                             axis=0) - u

    off += POOL_WIDTH
    mq = cols(off, MOBA_WIDTH) * (MOBA_HEAD_DIM ** -0.5 * LOG2_E)
    mqt_ref[0] = mq.T.astype(BF16)
    mk_ref[0] = cols(off + MOBA_WIDTH, MOBA_WIDTH).astype(BF16)
    mv = cols(off + 2 * MOBA_WIDTH, MOBA_WIDTH)
    ones = jnp.ones((MOBA_VT_ROWS - MOBA_HEAD_DIM, MOBA_BLOCK), F32)
    for j in range(tm // MOBA_BLOCK):
        mvt = mv[j * MOBA_BLOCK:(j + 1) * MOBA_BLOCK, :].T
        pieces = []
        for hd in range(MOBA_HEADS):
            pieces += [mvt[hd * MOBA_HEAD_DIM:(hd + 1) * MOBA_HEAD_DIM, :], ones]
        mvt_ref[0, j] = jnp.concatenate(pieces, axis=0).astype(BF16)

    y_pool = _dot(pooled.astype(BF16), poolw_ref[...]) * poolscale_ref[...]
    ypool_ref[0] = y_pool.astype(BF16)


def _proj(x, gain, w_in, cos, sin, pool_w, pool_scale, layer):
    b, s, _ = x.shape
    tm = TOKEN_TILE
    in_cols = w_in.shape[-1]
    nb = s // MOBA_BLOCK
    bpt = tm // MOBA_BLOCK
    tok = lambda width: pl.BlockSpec((1, tm, width), lambda bi, si: (bi, si, 0))
    out_shape = [
        jax.ShapeDtypeStruct((b, s, RET_QK), BF16),
        jax.ShapeDtypeStruct((b, s, RET_QK), BF16),
        jax.ShapeDtypeStruct((b, s, RET_WIDTH), BF16),
        jax.ShapeDtypeStruct((b, s, RET_WIDTH), BF16),
        jax.ShapeDtypeStruct((b, s, POOL_WIDTH), BF16),
        jax.ShapeDtypeStruct((b, MOBA_WIDTH, s), BF16),
        jax.ShapeDtypeStruct((b, s, MOBA_WIDTH), BF16),
        jax.ShapeDtypeStruct((b, nb, MOBA_HEADS * MOBA_VT_ROWS, MOBA_BLOCK), BF16),
    ]
    out_specs = [
        tok(RET_QK), tok(RET_QK), tok(RET_WIDTH), tok(RET_WIDTH), tok(POOL_WIDTH),
        pl.BlockSpec((1, MOBA_WIDTH, tm), lambda bi, si: (bi, 0, si)),
        tok(MOBA_WIDTH),
        pl.BlockSpec((1, bpt, MOBA_HEADS * MOBA_VT_ROWS, MOBA_BLOCK), lambda bi, si: (bi, si, 0, 0)),
    ]
    return pl.pallas_call(
        _proj_kernel,
        grid=(b, s // tm),
        in_specs=[
            pl.BlockSpec((1, tm, D_MODEL), lambda bi, si: (bi, si, 0)),
            _resident((None, 1, D_MODEL), lambda bi, si: (layer, 0, 0)),
            _resident((None, D_MODEL, in_cols), lambda bi, si: (layer, 0, 0)),
            pl.BlockSpec((tm, RET_QK), lambda bi, si: (si, 0)),
            pl.BlockSpec((tm, RET_QK), lambda bi, si: (si, 0)),
            _resident((None, POOL_WIDTH, POOL_WIDTH), lambda bi, si: (layer, 0, 0)),
            _resident((None, 1, POOL_WIDTH), lambda bi, si: (layer, 0, 0)),
        ],
        out_specs=out_specs,
        out_shape=out_shape,
        scratch_shapes=[pltpu.VMEM((POOL_HALO, POOL_WIDTH), F32)],
        compiler_params=_params("parallel", "arbitrary"),
        name="proj",
    )(x, gain, w_in, cos, sin, pool_w, pool_scale)


def _retention_kernel(q_ref, k_ref, v_ref, sg_ref, dmask_ref, xi_ref, zeta_ref, decay_ref,
                      ondiag_ref, o_ref, state_ref):
    @pl.when(pl.program_id(1) == 0)
    def _():
        state_ref[...] = jnp.zeros_like(state_ref)

    chunk = RET_CHUNK
    lane_head = lax.broadcasted_iota(jnp.int32, (chunk, RET_QK), 1) // RET_QK_DIM
    state = state_ref[...]
    for ci in range(q_ref.shape[1] // chunk):
        rows = slice(ci * chunk, (ci + 1) * chunk)
        q = q_ref[0, rows, :]
        k = k_ref[0, rows, :]
        v = v_ref[0, rows, :]
        kz = (k.astype(F32) * zeta_ref[...]).astype(BF16)
        update = _dot_tn(kz, v)
        cross = _dot(q, state.astype(BF16)) * xi_ref[...]
        qk = [_dot_nt(jnp.where(lane_head == hd, q, jnp.zeros_like(q)), k)
              for hd in range(RET_HEADS)]
        for hd in range(RET_HEADS):
            vcols = slice(hd * RET_V_DIM, (hd + 1) * RET_V_DIM)
            inner = (qk[hd] * dmask_ref[hd]).astype(BF16)
            o = _dot(inner, v[:, vcols]) + cross[:, vcols]
            mu = jnp.mean(o, axis=-1, keepdims=True)
            var = jnp.mean(jnp.square(o - mu), axis=-1, keepdims=True)
            o = (o - mu) * lax.rsqrt(var + EPS)
            o_ref[0, rows, vcols] = (o * sg_ref[0, rows, vcols].astype(F32)).astype(BF16)
        state = decay_ref[...] * state + ondiag_ref[...] * update
    state_ref[...] = state


def _retention(rq, rk, rv, sg, dmask, xi, zeta, decay, on_diag):
    b, s, _ = rq.shape
    c = RET_CHUNK * RET_STEP_CHUNKS
    tok = lambda width: pl.BlockSpec((1, c, width), lambda bi, ci: (bi, ci, 0))
    const = lambda a: _resident(a.shape, lambda bi, ci: (0,) * a.ndim)
    return pl.pallas_call(
        _retention_kernel,
        grid=(b, s // c),
        in_specs=[tok(RET_QK), tok(RET_QK), tok(RET_WIDTH), tok(RET_WIDTH),
                  const(dmask), const(xi), const(zeta), const(decay), const(on_diag)],
        out_specs=tok(RET_WIDTH),
        out_shape=jax.ShapeDtypeStruct((b, s, RET_WIDTH), BF16),
        scratch_shapes=[pltpu.VMEM((RET_QK, RET_WIDTH), F32)],
        compiler_params=_params("parallel", "arbitrary"),
        name="retention",
    )(rq, rk, rv, sg, dmask, xi, zeta, decay, on_diag)


def _retention_tables():
    c = RET_CHUNK
    log_gamma = jnp.log(1.0 - jnp.power(2.0, -5.0 - jnp.arange(RET_HEADS, dtype=F32)))
    idx = jnp.arange(c, dtype=F32)
    diff = idx[:, None] - idx[None, :]
    dmask = jnp.where(diff >= 0, jnp.exp(log_gamma[:, None, None] * jnp.maximum(diff, 0.0)), 0.0)
    xi = jnp.exp(log_gamma[:, None] * (idx + 1.0))
    zeta = jnp.exp(log_gamma[:, None] * (c - 1.0 - idx))
    g_chunk = jnp.exp(log_gamma * c)
    xi_t = jnp.repeat(xi.T, RET_V_DIM, axis=1)
    zeta_t = jnp.repeat(zeta.T, RET_QK_DIM, axis=1)
    head_of_row = np.arange(RET_QK) // RET_QK_DIM
    head_of_col = np.arange(RET_WIDTH) // RET_V_DIM
    on_diag = jnp.asarray(head_of_row[:, None] == head_of_col[None, :], dtype=F32)
    decay = on_diag * jnp.repeat(g_chunk, RET_QK_DIM)[:, None]
    return dmask, xi_t, zeta_t, decay, on_diag


def _rotation_tables(seq):
    pos = jnp.arange(seq, dtype=F32)
    angle = 1.0 / (10000.0 ** jnp.linspace(0.0, 1.0, RET_QK_DIM // 2, dtype=F32))
    angle = jnp.repeat(angle, 2)
    ang = pos[:, None] * angle[None, :]
    return jnp.tile(jnp.cos(ang), (1, RET_HEADS)), jnp.tile(jnp.sin(ang), (1, RET_HEADS))


def _moba_kernel(qt_ref, k_ref, vt_ref, o_ref,
                 kmean_ref, kabs_ref, qh_ref, shift_ref, bound_ref, m_ref, acc_ref, p_ref,
                 s_even_ref, s_odd_ref, top_even_ref, top_odd_ref):
    qi = pl.program_id(1)
    nb = kmean_ref.shape[0]
    blk = MOBA_BLOCK
    tq = qt_ref.shape[2]
    heads = range(MOBA_HEADS)
    vt_rows = [slice(hd * MOBA_VT_ROWS, (hd + 1) * MOBA_VT_ROWS) for hd in heads]
    chunks = [slice(r, r + MOBA_ROW_CHUNK) for r in range(0, blk, MOBA_ROW_CHUNK)]

    @pl.when(qi == 0)
    def _():
        for j in range(nb):
            kb = k_ref[0, j * blk:(j + 1) * blk, :].astype(F32)
            kmean_ref[j:j + 1, :] = jnp.mean(kb, axis=0, keepdims=True)
            kabs_ref[j:j + 1, :] = jnp.max(jnp.abs(kb), axis=0, keepdims=True)

    qt = qt_ref[0]
    row_head = lax.broadcasted_iota(jnp.int32, (MOBA_WIDTH, tq), 0) // MOBA_HEAD_DIM
    for hd in heads:
        qh_ref[hd] = jnp.where(row_head == hd, qt, jnp.zeros_like(qt))

    def key_block(j):
        return k_ref[0, pl.ds(pl.multiple_of(j * blk, blk), blk), :]

    def stage_scores(j, s_ref, heads):
        kj = key_block(j)
        for hd in heads:
            s_ref[hd] = _dot(kj, qh_ref[hd])

    kmean = kmean_ref[...].astype(BF16)
    kabs = (kabs_ref[...] * MOBA_BOUND_SLACK).astype(BF16)
    gates = [_dot(kmean, qh_ref[hd]) for hd in heads]
    score_bounds = [_dot(kabs, jnp.abs(qh_ref[hd])) for hd in heads]
    stage_scores(0, s_even_ref, heads)

    blk_id = lax.broadcasted_iota(jnp.int32, (nb, tq), 0)
    past = blk_id < qi
    for hd in heads:
        gate = jnp.where(past, gates[hd], NEG_INF)
        chosen = jnp.zeros((nb, tq), jnp.bool_)
        for _ in range(MOBA_TOPK):
            best = jnp.max(gate, axis=0, keepdims=True)
            idx = jnp.min(jnp.where(gate == best, blk_id, nb), axis=0, keepdims=True)
            hit = blk_id == idx
            chosen = jnp.logical_or(chosen, jnp.logical_and(hit, past))
            gate = jnp.where(hit, NEG_INF, gate)
        attended = jnp.logical_or(chosen, blk_id == qi)
        bound = jnp.max(jnp.where(attended, score_bounds[hd], NEG_INF), axis=0, keepdims=True)
        bound_ref[hd:hd + 1, :] = bound
        shift_ref[hd] = jnp.where(chosen, bound, float("inf"))

    def causal(s, row0):
        key_pos = row0 + lax.broadcasted_iota(jnp.int32, s.shape, 0)
        qry_pos = lax.broadcasted_iota(jnp.int32, s.shape, 1)
        return jnp.where(key_pos <= qry_pos, s, NEG_INF)

    def finish():
        out = []
        for hd in heads:
            base = hd * MOBA_VT_ROWS
            denom = acc_ref[base + MOBA_HEAD_DIM:base + MOBA_HEAD_DIM + 1, :]
            out.append(acc_ref[base:base + MOBA_HEAD_DIM, :] / denom)
        o_ref[0] = jnp.concatenate(out, axis=0).T.astype(BF16)

    def stage_values(j, s_ref, slot, heads, own=False):
        for hd in heads:
            shift = bound_ref[hd:hd + 1, :] if own else shift_ref[hd, pl.ds(j, 1), :]
            for c in chunks:
                s = s_ref[hd, c, :]
                s = causal(s, c.start) if own else s
                p_ref[slot, hd, c, :] = jnp.exp2(s - shift).astype(BF16)
            pv = _dot(vt_ref[0, j, vt_rows[hd], :], p_ref[slot, hd])
            acc_ref[vt_rows[hd], :] = acc_ref[vt_rows[hd], :] + pv

    acc_ref[...] = jnp.zeros_like(acc_ref)
    head_groups = [(0, 1), (2, 3)]

    def trips(first_block, n_blocks):
        def body(t, carry):
            for i in range(n_blocks):
                j = jnp.minimum(first_block + n_blocks * t + i, qi)
                cur, nxt = (s_even_ref, s_odd_ref) if i % 2 == 0 else (s_odd_ref, s_even_ref)
                for g in head_groups:
                    stage_scores(jnp.minimum(j + 1, qi), nxt, g)
                    stage_values(j, cur, i % 2, g)
            return carry
        return body

    covered = 0
    for n_blocks in MOBA_TRIP_BLOCKS[:-1]:
        n_trips = (qi - covered) // n_blocks
        lax.fori_loop(0, n_trips, trips(covered, n_blocks), 0)
        covered = covered + n_trips * n_blocks
    lax.fori_loop(0, (qi - covered + 1) // 2, trips(covered, MOBA_TRIP_BLOCKS[-1]), 0)
    stage_values(qi, s_even_ref, 0, heads, own=True)

    sums = [acc_ref[hd * MOBA_VT_ROWS + MOBA_HEAD_DIM:hd * MOBA_VT_ROWS + MOBA_HEAD_DIM + 1, :]
            for hd in heads]
    in_range = jnp.min(jnp.concatenate(sums, axis=0)) >= MOBA_MIN_SUM
    finish()

    def col_max(tiles):
        top = tiles[0]
        for t in tiles[1:]:
            top = jnp.maximum(top, t)
        return jnp.max(top, axis=0, keepdims=True)

    def scores(j, buf, heads=heads):
        s_ref, top_ref = buf
        kj = key_block(j)
        for hd in heads:
            s = _dot(kj, qh_ref[hd])
            s_ref[hd] = s
            top_ref[hd:hd + 1, :] = col_max([s[c] for c in chunks])

    def attend(j, buf, own, heads=heads):
        s_ref, top_ref = buf
        for hd in heads:
            def chunk(c):
                s = s_ref[hd, c, :]
                return causal(s, c.start) if own else s

            m_old = m_ref[hd:hd + 1, :]
            if own:
                m_new = jnp.maximum(m_old, col_max([chunk(c) for c in chunks]))
                shift = m_new
            else:
                pen = shift_ref[hd, pl.ds(j, 1), :] - bound_ref[hd:hd + 1, :]
                m_new = jnp.maximum(m_old, top_ref[hd:hd + 1, :] - pen)
                shift = m_new + pen
            m_ref[hd:hd + 1, :] = m_new
            for c in chunks:
                p_ref[0, hd, c, :] = jnp.exp2((chunk(c) - shift).astype(BF16))
            pv = _dot(vt_ref[0, j, vt_rows[hd], :], p_ref[0, hd])
            acc_ref[vt_rows[hd], :] = jnp.exp2(m_old - m_new) * acc_ref[vt_rows[hd], :] + pv

    @pl.when(jnp.logical_not(in_range))
    def _():
        m_ref[...] = jnp.full(m_ref.shape, MOBA_M_INIT, F32)
        acc_ref[...] = jnp.zeros_like(acc_ref)
        even = (s_even_ref, top_even_ref)
        odd = (s_odd_ref, top_odd_ref)
        scores(0, even)

        def body(t, carry):
            j = 2 * t
            scores(j + 1, odd)
            attend(j, even, own=False)
            scores(jnp.minimum(j + 2, qi), even)
            attend(j + 1, odd, own=False)
            return carry

        lax.fori_loop(0, (qi + 1) // 2, body, 0)
        attend(qi, even, own=True)
        finish()


def _moba(mqt, mk, mvt):
    b, s, _ = mk.shape
    nb = s // MOBA_BLOCK
    tq = MOBA_BLOCK
    vt_rows = MOBA_HEADS * MOBA_VT_ROWS
    block_stat_buf = pltpu.VMEM((nb, MOBA_WIDTH), F32)
    scores_buf = pltpu.VMEM((MOBA_HEADS, MOBA_BLOCK, tq), F32)
    head_rows_buf = pltpu.VMEM((8, tq), F32)
    return pl.pallas_call(
        _moba_kernel,
        grid=(b, nb),
        in_specs=[
            pl.BlockSpec((1, MOBA_WIDTH, tq), lambda bi, qi: (bi, 0, qi)),
            pl.BlockSpec((1, s, MOBA_WIDTH), lambda bi, qi: (bi, 0, 0)),
            pl.BlockSpec((1, nb, vt_rows, MOBA_BLOCK), lambda bi, qi: (bi, 0, 0, 0)),
        ],
        out_specs=pl.BlockSpec((1, tq, MOBA_WIDTH), lambda bi, qi: (bi, qi, 0)),
        out_shape=jax.ShapeDtypeStruct((b, s, MOBA_WIDTH), BF16),
        scratch_shapes=[
            block_stat_buf, block_stat_buf,
            pltpu.VMEM((MOBA_HEADS, MOBA_WIDTH, tq), BF16),
            pltpu.VMEM((MOBA_HEADS, nb, tq), F32),
            head_rows_buf,
            head_rows_buf,
            pltpu.VMEM((vt_rows, tq), F32),
            pltpu.VMEM((2, MOBA_HEADS, MOBA_BLOCK, tq), BF16),
            scores_buf, scores_buf,
            head_rows_buf, head_rows_buf,
        ],
        compiler_params=_params("parallel", "arbitrary"),
        name="moba",
    )(mqt, mk, mvt)


def _tail_kernel(x_ref, yr_ref, yp_ref, ym_ref, p_ref, wout_ref,
                 ffn_gain_ref, wg_ref, wu_ref, wd_ref,
                 ple_gain_ref, ple_wgate_ref, ple_wproj_ref, final_gain_ref,
                 o_ref, act_ref, *, final):
    x = x_ref[...]
    x = x + _dot(yr_ref[...], wout_ref[:RET_WIDTH, :])
    x = x + _dot(yp_ref[...], wout_ref[RET_WIDTH:RET_WIDTH + POOL_WIDTH, :])
    x = x + _dot(ym_ref[...], wout_ref[RET_WIDTH + POOL_WIDTH:, :])
    emb = _dot(p_ref[...].astype(BF16), ple_wproj_ref[...])
    x = _swiglu_half_step(x, ffn_gain_ref[...], wg_ref, wu_ref, wd_ref, act_ref)
    h = _rms(x, ple_gain_ref[...]).astype(BF16)
    x = x + jax.nn.sigmoid(_dot(h, ple_wgate_ref[...])) * emb
    if final:
        x = _rms(x, final_gain_ref[...])
    o_ref[...] = x


def _tail(x, y_ret, y_pool, y_moba, p, w_out, ffn, ple_gain, ple_w_gate, ple_w_proj,
          final_gain, layer, final):
    tokens = x.shape[0]
    tm = TOKEN_TILE
    ffn_gain, wg, wu, wd = ffn
    tok = lambda width: pl.BlockSpec((tm, width), lambda i: (i, 0))
    per_layer = lambda *shape: _resident((None,) + shape, lambda i: (layer,) + (0,) * len(shape))
    return pl.pallas_call(
        functools.partial(_tail_kernel, final=final),
        grid=(tokens // tm,),
        in_specs=[
            tok(D_MODEL), tok(RET_WIDTH), tok(POOL_WIDTH), tok(MOBA_WIDTH),
            pl.BlockSpec((None, tm, PLE_DIM), lambda i: (layer, i, 0)),
            per_layer(D_MODEL, D_MODEL),
            per_layer(1, D_MODEL), per_layer(D_MODEL, D_FF), per_layer(D_MODEL, D_FF),
            per_layer(D_FF, D_MODEL),
            per_layer(1, D_MODEL), per_layer(D_MODEL, D_MODEL), per_layer(PLE_DIM, D_MODEL),
            _resident((1, D_MODEL), lambda i: (0, 0)),
        ],
        out_specs=tok(D_MODEL),
        out_shape=jax.ShapeDtypeStruct(x.shape, F32),
        scratch_shapes=[pltpu.VMEM((tm, D_FF), BF16)],
        compiler_params=_params("parallel"),
        name="tail",
    )(x, y_ret, y_pool, y_moba, p, w_out, ffn_gain, wg, wu, wd,
      ple_gain, ple_w_gate, ple_w_proj, final_gain)


def _block_diag(w):
    depth, groups, d, _ = w.shape
    eye = jnp.eye(groups, dtype=w.dtype)
    return jnp.einsum("lgcd,gh->lgchd", w, eye).reshape(depth, groups * d, groups * d)


def kernel(x, p, norm_ffn1, ffn1_w_gate, ffn1_w_up, ffn1_w_down, norm_mix, w_in, pool_w, pool_scale, w_out, norm_ffn2, ffn2_w_gate, ffn2_w_up, ffn2_w_down, norm_ple, ple_w_gate, ple_w_proj, norm_final):
    b, s, d = x.shape
    depth = w_in.shape[0]
    tokens = b * s
    bf = lambda w: w.astype(BF16)
    row = lambda g: g.reshape(g.shape[0], 1, g.shape[1])

    ffn1 = (row(norm_ffn1), bf(ffn1_w_gate), bf(ffn1_w_up), bf(ffn1_w_down))
    ffn2 = (row(norm_ffn2), bf(ffn2_w_gate), bf(ffn2_w_up), bf(ffn2_w_down))
    w_in_b, w_out_b = bf(w_in), bf(w_out)
    pool_w_b = bf(_block_diag(pool_w))
    ple_gate_b, ple_proj_b = bf(ple_w_gate), bf(ple_w_proj)
    norm_mix_r, norm_ple_r, pool_scale_r = row(norm_mix), row(norm_ple), row(pool_scale)
    final_gain = norm_final.reshape(1, d)
    p_flat = p.reshape(depth, tokens, PLE_DIM)
    cos, sin = _rotation_tables(s)
    ret_tables = _retention_tables()

    xf = x.reshape(tokens, d)
    for i in range(depth):
        xf = _ffn(xf, *ffn1, i)
        rq, rk, rv, sg, y_pool, mqt, mk, mvt = _proj(
            xf.reshape(b, s, d), norm_mix_r, w_in_b, cos, sin, pool_w_b, pool_scale_r, i)
        y_ret = _retention(rq, rk, rv, sg, *ret_tables)
        y_moba = _moba(mqt, mk, mvt)
        xf = _tail(xf, y_ret.reshape(tokens, RET_WIDTH), y_pool.reshape(tokens, POOL_WIDTH),
                   y_moba.reshape(tokens, MOBA_WIDTH), p_flat, w_out_b, ffn2,
                   norm_ple_r, ple_gate_b, ple_proj_b, final_gain, i, i == depth - 1)
    return xf.reshape(b, s, d)
```

```python
import functools

import jax
import jax.numpy as jnp
import numpy as np
from jax import lax
from jax.experimental import pallas as pl
from jax.experimental.pallas import tpu as pltpu

D_MODEL = 1024
D_FF = 2816
RET_HEADS = 4
RET_QK_DIM = 64
RET_V_DIM = 128
RET_CHUNK = 256
RET_STEP_CHUNKS = 8
RET_QK = RET_HEADS * RET_QK_DIM
RET_WIDTH = RET_HEADS * RET_V_DIM
POOL_GROUPS = 4
POOL_WINDOWS = (2, 4, 8, 16)
POOL_GROUP_DIM = 64
POOL_WIDTH = POOL_GROUPS * POOL_GROUP_DIM
POOL_HALO = 16
assert POOL_WINDOWS == tuple(2 ** (i + 1) for i in range(POOL_GROUPS)) and POOL_HALO == POOL_WINDOWS[-1]
MOBA_HEADS = 4
MOBA_HEAD_DIM = 64
MOBA_WIDTH = MOBA_HEADS * MOBA_HEAD_DIM
MOBA_BLOCK = 256
MOBA_TOPK = 3
MOBA_VT_ROWS = MOBA_HEAD_DIM + 16
MOBA_ROW_CHUNK = 64
MOBA_M_INIT = -(2.0 ** 100)
MOBA_TRIP_BLOCKS = (16, 8, 4, 2)
assert MOBA_TRIP_BLOCKS[-1] == 2 and all(n % 2 == 0 for n in MOBA_TRIP_BLOCKS)
MOBA_BOUND_SLACK = 1.0 + 2.0 ** -7
MOBA_MIN_SUM = 2.0 ** -100
PLE_DIM = 256
EPS = 1e-6

FF_CHUNK = 256
TOKEN_TILE = 512
VMEM_LIMIT = 56 * 1024 * 1024

BF16 = jnp.bfloat16
F32 = jnp.float32
NEG_INF = float("-inf")
LOG2_E = 1.4426950408889634


def _params(*semantics):
    return pltpu.CompilerParams(dimension_semantics=semantics, vmem_limit_bytes=VMEM_LIMIT)


def _resident(shape, index_map):
    return pl.BlockSpec(shape, index_map, pipeline_mode=pl.Buffered(1))


def _rms(x, gain):
    return x * lax.rsqrt(jnp.mean(x * x, axis=-1, keepdims=True) + EPS) * gain


def _dot(a, b):
    return jnp.dot(a, b, preferred_element_type=F32)


def _dot_nt(a, b):
    return lax.dot_general(a, b, (((1,), (1,)), ((), ())), preferred_element_type=F32)


def _dot_tn(a, b):
    return lax.dot_general(a, b, (((0,), (0,)), ((), ())), preferred_element_type=F32)


def _swiglu_half_step(x, gain, wg_ref, wu_ref, wd_ref, act_ref):
    h = _rms(x, gain).astype(BF16)
    for c in range(D_FF // FF_CHUNK):
        cols = slice(c * FF_CHUNK, (c + 1) * FF_CHUNK)
        gate = _dot(h, wg_ref[:, cols])
        up = _dot(h, wu_ref[:, cols])
        act_ref[:, cols] = (gate * jax.nn.sigmoid(gate) * up).astype(BF16)
    return x + 0.5 * _dot(act_ref[...], wd_ref[...])


def _ffn_kernel(x_ref, gain_ref, wg_ref, wu_ref, wd_ref, o_ref, act_ref):
    half = x_ref.shape[0] // 2
    rows = [slice(0, half), slice(half, 2 * half)]
    gain = gain_ref[...]
    h = [_rms(x_ref[rows[0], :], gain).astype(BF16), None]
    for part in range(2):
        for c in range(D_FF // FF_CHUNK):
            cols = slice(c * FF_CHUNK, (c + 1) * FF_CHUNK)
            gate = _dot(h[part], wg_ref[:, cols])
            up = _dot(h[part], wu_ref[:, cols])
            act_ref[rows[part], cols] = (gate * jax.nn.sigmoid(gate) * up).astype(BF16)
            if part == 0 and c == 0:
                h[1] = _rms(x_ref[rows[1], :], gain).astype(BF16)
    for part in range(2):
        down = _dot(act_ref[rows[part], :], wd_ref[...])
        o_ref[rows[part], :] = x_ref[rows[part], :] + 0.5 * down


def _ffn(x, gain, wg, wu, wd, layer):
    tokens = x.shape[0]
    tm = 2 * TOKEN_TILE
    return pl.pallas_call(
        _ffn_kernel,
        grid=(tokens // tm,),
        in_specs=[
            pl.BlockSpec((tm, D_MODEL), lambda i: (i, 0)),
            _resident((None, 1, D_MODEL), lambda i: (layer, 0, 0)),
            _resident((None, D_MODEL, D_FF), lambda i: (layer, 0, 0)),
            _resident((None, D_MODEL, D_FF), lambda i: (layer, 0, 0)),
            _resident((None, D_FF, D_MODEL), lambda i: (layer, 0, 0)),
        ],
        out_specs=pl.BlockSpec((tm, D_MODEL), lambda i: (i, 0)),
        out_shape=jax.ShapeDtypeStruct(x.shape, F32),
        scratch_shapes=[pltpu.VMEM((tm, D_FF), BF16)],
        compiler_params=_params("parallel"),
        name="ffn",
    )(x, gain, wg, wu, wd)


def _rotate_every_two(t, even_lane):
    n = t.shape[-1]
    nxt = pltpu.roll(t, n - 1, 1)
    prv = pltpu.roll(t, 1, 1)
    return jnp.where(even_lane, -nxt, prv)


def _proj_kernel(x_ref, gain_ref, w_ref, cos_ref, sin_ref, poolw_ref, poolscale_ref,
                 rq_ref, rk_ref, rv_ref, sg_ref, ypool_ref, mqt_ref, mk_ref, mvt_ref,
                 halo_ref):
    tm = x_ref.shape[1]
    st = pl.program_id(1)
    h = _rms(x_ref[0], gain_ref[...]).astype(BF16)

    def cols(lo, width):
        return _dot(h, w_ref[:, lo:lo + width])

    cos = cos_ref[...]
    sin = sin_ref[...]
    even_lane = (lax.broadcasted_iota(jnp.int32, (tm, RET_QK), 1) % 2) == 0
    q = cols(0, RET_QK)
    rq_ref[0] = (q * cos + _rotate_every_two(q, even_lane) * sin).astype(BF16)
    k = cols(RET_QK, RET_QK)
    k = (k * cos + _rotate_every_two(k, even_lane) * sin) * (RET_QK_DIM ** -0.5)
    rk_ref[0] = k.astype(BF16)
    rv_ref[0] = cols(2 * RET_QK, RET_WIDTH).astype(BF16)
    g = cols(2 * RET_QK + RET_WIDTH, RET_WIDTH)
    sg_ref[0] = (g * jax.nn.sigmoid(g)).astype(BF16)

    off = 2 * RET_QK + 2 * RET_WIDTH
    u = cols(off, POOL_WIDTH)

    first_tile = st == 0
    halo = jnp.where(first_tile, 0.0, halo_ref[...])
    ext = jnp.concatenate([halo, u], axis=0)
    halo_ref[...] = u[tm - POOL_HALO:, :]
    group = lax.broadcasted_iota(jnp.int32, (tm, POOL_WIDTH), 1) // POOL_GROUP_DIM
    group_row = lax.broadcasted_iota(jnp.int32, (1, POOL_WIDTH), 1) // POOL_GROUP_DIM
    wsum = None
    window = None
    acc = ext
    for gi, win in enumerate(POOL_WINDOWS):
        acc = acc + pltpu.roll(acc, win // 2, 0)
        cur = acc[POOL_HALO:, :]
        wsum = cur if wsum is None else jnp.where(group == gi, cur, wsum)
        wl = jnp.full((1, POOL_WIDTH), float(win), F32)
        window = wl if window is None else jnp.where(group_row == gi, wl, window)

    pos = (lax.broadcasted_iota(jnp.int32, (POOL_HALO, POOL_WIDTH), 0) + 1).astype(F32)
    short = jnp.where(first_tile, jnp.minimum(pos, window), window)
    pooled = jnp.concatenate([wsum[:POOL_HALO] / short, wsum[POOL_HALO:] * (1.0 / window)],
                             axis=0) - u

    off += POOL_WIDTH
    mq = cols(off, MOBA_WIDTH) * (MOBA_HEAD_DIM ** -0.5 * LOG2_E)
    mqt_ref[0] = mq.T.astype(BF16)
    mk_ref[0] = cols(off + MOBA_WIDTH, MOBA_WIDTH).astype(BF16)
    mv = cols(off + 2 * MOBA_WIDTH, MOBA_WIDTH)
    ones = jnp.ones((MOBA_VT_ROWS - MOBA_HEAD_DIM, MOBA_BLOCK), F32)
    for j in range(tm // MOBA_BLOCK):
        mvt = mv[j * MOBA_BLOCK:(j + 1) * MOBA_BLOCK, :].T
        pieces = []
        for hd in range(MOBA_HEADS):
            pieces += [mvt[hd * MOBA_HEAD_DIM:(hd + 1) * MOBA_HEAD_DIM, :], ones]
        mvt_ref[0, j] = jnp.concatenate(pieces, axis=0).astype(BF16)

    y_pool = _dot(pooled.astype(BF16), poolw_ref[...]) * poolscale_ref[...]
    ypool_ref[0] = y_pool.astype(BF16)


def _proj(x, gain, w_in, cos, sin, pool_w, pool_scale, layer):
    b, s, _ = x.shape
    tm = TOKEN_TILE
    in_cols = w_in.shape[-1]
    nb = s // MOBA_BLOCK
    bpt = tm // MOBA_BLOCK
    tok = lambda width: pl.BlockSpec((1, tm, width), lambda bi, si: (bi, si, 0))
    out_shape = [
        jax.ShapeDtypeStruct((b, s, RET_QK), BF16),
        jax.ShapeDtypeStruct((b, s, RET_QK), BF16),
        jax.ShapeDtypeStruct((b, s, RET_WIDTH), BF16),
        jax.ShapeDtypeStruct((b, s, RET_WIDTH), BF16),
        jax.ShapeDtypeStruct((b, s, POOL_WIDTH), BF16),
        jax.ShapeDtypeStruct((b, MOBA_WIDTH, s), BF16),
        jax.ShapeDtypeStruct((b, s, MOBA_WIDTH), BF16),
        jax.ShapeDtypeStruct((b, nb, MOBA_HEADS * MOBA_VT_ROWS, MOBA_BLOCK), BF16),
    ]
    out_specs = [
        tok(RET_QK), tok(RET_QK), tok(RET_WIDTH), tok(RET_WIDTH), tok(POOL_WIDTH),
        pl.BlockSpec((1, MOBA_WIDTH, tm), lambda bi, si: (bi, 0, si)),
        tok(MOBA_WIDTH),
        pl.BlockSpec((1, bpt, MOBA_HEADS * MOBA_VT_ROWS, MOBA_BLOCK), lambda bi, si: (bi, si, 0, 0)),
    ]
    return pl.pallas_call(
        _proj_kernel,
        grid=(b, s // tm),
        in_specs=[
            pl.BlockSpec((1, tm, D_MODEL), lambda bi, si: (bi, si, 0)),
            _resident((None, 1, D_MODEL), lambda bi, si: (layer, 0, 0)),
            _resident((None, D_MODEL, in_cols), lambda bi, si: (layer, 0, 0)),
            pl.BlockSpec((tm, RET_QK), lambda bi, si: (si, 0)),
            pl.BlockSpec((tm, RET_QK), lambda bi, si: (si, 0)),
            _resident((None, POOL_WIDTH, POOL_WIDTH), lambda bi, si: (layer, 0, 0)),
            _resident((None, 1, POOL_WIDTH), lambda bi, si: (layer, 0, 0)),
        ],
        out_specs=out_specs,
        out_shape=out_shape,
        scratch_shapes=[pltpu.VMEM((POOL_HALO, POOL_WIDTH), F32)],
        compiler_params=_params("parallel", "arbitrary"),
        name="proj",
    )(x, gain, w_in, cos, sin, pool_w, pool_scale)


def _retention_kernel(q_ref, k_ref, v_ref, sg_ref, dmask_ref, xi_ref, zeta_ref, decay_ref,
                      ondiag_ref, o_ref, state_ref):
    @pl.when(pl.program_id(1) == 0)
    def _():
        state_ref[...] = jnp.zeros_like(state_ref)

    chunk = RET_CHUNK
    lane_head = lax.broadcasted_iota(jnp.int32, (chunk, RET_QK), 1) // RET_QK_DIM
    state = state_ref[...]
    for ci in range(q_ref.shape[1] // chunk):
        rows = slice(ci * chunk, (ci + 1) * chunk)
        q = q_ref[0, rows, :]
        k = k_ref[0, rows, :]
        v = v_ref[0, rows, :]
        kz = (k.astype(F32) * zeta_ref[...]).astype(BF16)
        update = _dot_tn(kz, v)
        cross = _dot(q, state.astype(BF16)) * xi_ref[...]
        qk = [_dot_nt(jnp.where(lane_head == hd, q, jnp.zeros_like(q)), k)
              for hd in range(RET_HEADS)]
        for hd in range(RET_HEADS):
            vcols = slice(hd * RET_V_DIM, (hd + 1) * RET_V_DIM)
            inner = (qk[hd] * dmask_ref[hd]).astype(BF16)
            o = _dot(inner, v[:, vcols]) + cross[:, vcols]
            mu = jnp.mean(o, axis=-1, keepdims=True)
            var = jnp.mean(jnp.square(o - mu), axis=-1, keepdims=True)
            o = (o - mu) * lax.rsqrt(var + EPS)
            o_ref[0, rows, vcols] = (o * sg_ref[0, rows, vcols].astype(F32)).astype(BF16)
        state = decay_ref[...] * state + ondiag_ref[...] * update
    state_ref[...] = state


def _retention(rq, rk, rv, sg, dmask, xi, zeta, decay, on_diag):
    b, s, _ = rq.shape
    c = RET_CHUNK * RET_STEP_CHUNKS
    tok = lambda width: pl.BlockSpec((1, c, width), lambda bi, ci: (bi, ci, 0))
    const = lambda a: _resident(a.shape, lambda bi, ci: (0,) * a.ndim)
    return pl.pallas_call(
        _retention_kernel,
        grid=(b, s // c),
        in_specs=[tok(RET_QK), tok(RET_QK), tok(RET_WIDTH), tok(RET_WIDTH),
                  const(dmask), const(xi), const(zeta), const(decay), const(on_diag)],
        out_specs=tok(RET_WIDTH),
        out_shape=jax.ShapeDtypeStruct((b, s, RET_WIDTH), BF16),
        scratch_shapes=[pltpu.VMEM((RET_QK, RET_WIDTH), F32)],
        compiler_params=_params("parallel", "arbitrary"),
        name="retention",
    )(rq, rk, rv, sg, dmask, xi, zeta, decay, on_diag)


def _retention_tables():
    c = RET_CHUNK
    log_gamma = jnp.log(1.0 - jnp.power(2.0, -5.0 - jnp.arange(RET_HEADS, dtype=F32)))
    idx = jnp.arange(c, dtype=F32)
    diff = idx[:, None] - idx[None, :]
    dmask = jnp.where(diff >= 0, jnp.exp(log_gamma[:, None, None] * jnp.maximum(diff, 0.0)), 0.0)
    xi = jnp.exp(log_gamma[:, None] * (idx + 1.0))
    zeta = jnp.exp(log_gamma[:, None] * (c - 1.0 - idx))
    g_chunk = jnp.exp(log_gamma * c)
    xi_t = jnp.repeat(xi.T, RET_V_DIM, axis=1)
    zeta_t = jnp.repeat(zeta.T, RET_QK_DIM, axis=1)
    head_of_row = np.arange(RET_QK) // RET_QK_DIM
    head_of_col = np.arange(RET_WIDTH) // RET_V_DIM
    on_diag = jnp.asarray(head_of_row[:, None] == head_of_col[None, :], dtype=F32)
    decay = on_diag * jnp.repeat(g_chunk, RET_QK_DIM)[:, None]
    return dmask, xi_t, zeta_t, decay, on_diag


def _rotation_tables(seq):
    pos = jnp.arange(seq, dtype=F32)
    angle = 1.0 / (10000.0 ** jnp.linspace(0.0, 1.0, RET_QK_DIM // 2, dtype=F32))
    angle = jnp.repeat(angle, 2)
    ang = pos[:, None] * angle[None, :]
    return jnp.tile(jnp.cos(ang), (1, RET_HEADS)), jnp.tile(jnp.sin(ang), (1, RET_HEADS))


def _moba_kernel(qt_ref, k_ref, vt_ref, o_ref,
                 kmean_ref, kabs_ref, qh_ref, shift_ref, bound_ref, m_ref, acc_ref, p_ref,
                 s_even_ref, s_odd_ref, top_even_ref, top_odd_ref):
    qi = pl.program_id(1)
    nb = kmean_ref.shape[0]
    blk = MOBA_BLOCK
    tq = qt_ref.shape[2]
    heads = range(MOBA_HEADS)
    vt_rows = [slice(hd * MOBA_VT_ROWS, (hd + 1) * MOBA_VT_ROWS) for hd in heads]
    chunks = [slice(r, r + MOBA_ROW_CHUNK) for r in range(0, blk, MOBA_ROW_CHUNK)]

    @pl.when(qi == 0)
    def _():
        for j in range(nb):
            kb = k_ref[0, j * blk:(j + 1) * blk, :].astype(F32)
            kmean_ref[j:j + 1, :] = jnp.mean(kb, axis=0, keepdims=True)
            kabs_ref[j:j + 1, :] = jnp.max(jnp.abs(kb), axis=0, keepdims=True)

    qt = qt_ref[0]
    row_head = lax.broadcasted_iota(jnp.int32, (MOBA_WIDTH, tq), 0) // MOBA_HEAD_DIM
    for hd in heads:
        qh_ref[hd] = jnp.where(row_head == hd, qt, jnp.zeros_like(qt))

    def key_block(j):
        return k_ref[0, pl.ds(pl.multiple_of(j * blk, blk), blk), :]

    def stage_scores(j, s_ref, heads):
        kj = key_block(j)
        for hd in heads:
            s_ref[hd] = _dot(kj, qh_ref[hd])

    kmean = kmean_ref[...].astype(BF16)
    kabs = (kabs_ref[...] * MOBA_BOUND_SLACK).astype(BF16)
    gates = [_dot(kmean, qh_ref[hd]) for hd in heads]
    score_bounds = [_dot(kabs, jnp.abs(qh_ref[hd])) for hd in heads]
    stage_scores(0, s_even_ref, heads)

    blk_id = lax.broadcasted_iota(jnp.int32, (nb, tq), 0)
    past = blk_id < qi
    for hd in heads:
        gate = jnp.where(past, gates[hd], NEG_INF)
        chosen = jnp.zeros((nb, tq), jnp.bool_)
        for _ in range(MOBA_TOPK):
            best = jnp.max(gate, axis=0, keepdims=True)
            idx = jnp.min(jnp.where(gate == best, blk_id, nb), axis=0, keepdims=True)
            hit = blk_id == idx
            chosen = jnp.logical_or(chosen, jnp.logical_and(hit, past))
            gate = jnp.where(hit, NEG_INF, gate)
        attended = jnp.logical_or(chosen, blk_id == qi)
        bound = jnp.max(jnp.where(attended, score_bounds[hd], NEG_INF), axis=0, keepdims=True)
        bound_ref[hd:hd + 1, :] = bound
        shift_ref[hd] = jnp.where(chosen, bound, float("inf"))

    def causal(s, row0):
        key_pos = row0 + lax.broadcasted_iota(jnp.int32, s.shape, 0)
        qry_pos = lax.broadcasted_iota(jnp.int32, s.shape, 1)
        return jnp.where(key_pos <= qry_pos, s, NEG_INF)

    def finish():
        out = []
        for hd in heads:
            base = hd * MOBA_VT_ROWS
            denom = acc_ref[base + MOBA_HEAD_DIM:base + MOBA_HEAD_DIM + 1, :]
            out.append(acc_ref[base:base + MOBA_HEAD_DIM, :] / denom)
        o_ref[0] = jnp.concatenate(out, axis=0).T.astype(BF16)

    def stage_values(j, s_ref, slot, heads, own=False):
        for hd in heads:
            shift = bound_ref[hd:hd + 1, :] if own else shift_ref[hd, pl.ds(j, 1), :]
            for c in chunks:
                s = s_ref[hd, c, :]
                s = causal(s, c.start) if own else s
                p_ref[slot, hd, c, :] = jnp.exp2(s - shift).astype(BF16)
            pv = _dot(vt_ref[0, j, vt_rows[hd], :], p_ref[slot, hd])
            acc_ref[vt_rows[hd], :] = acc_ref[vt_rows[hd], :] + pv

    acc_ref[...] = jnp.zeros_like(acc_ref)
    head_groups = [(0, 1), (2, 3)]

    def trips(first_block, n_blocks):
        def body(t, carry):
            for i in range(n_blocks):
                j = jnp.minimum(first_block + n_blocks * t + i, qi)
                cur, nxt = (s_even_ref, s_odd_ref) if i % 2 == 0 else (s_odd_ref, s_even_ref)
                for g in head_groups:
                    stage_scores(jnp.minimum(j + 1, qi), nxt, g)
                    stage_values(j, cur, i % 2, g)
            return carry
        return body

    covered = 0
    for n_blocks in MOBA_TRIP_BLOCKS[:-1]:
        n_trips = (qi - covered) // n_blocks
        lax.fori_loop(0, n_trips, trips(covered, n_blocks), 0)
        covered = covered + n_trips * n_blocks
    lax.fori_loop(0, (qi - covered + 1) // 2, trips(covered, MOBA_TRIP_BLOCKS[-1]), 0)
    stage_values(qi, s_even_ref, 0, heads, own=True)

    sums = [acc_ref[hd * MOBA_VT_ROWS + MOBA_HEAD_DIM:hd * MOBA_VT_ROWS + MOBA_HEAD_DIM + 1, :]
            for hd in heads]
    in_range = jnp.min(jnp.concatenate(sums, axis=0)) >= MOBA_MIN_SUM
    finish()

    def col_max(tiles):
        top = tiles[0]
        for t in tiles[1:]:
            top = jnp.maximum(top, t)
        return jnp.max(top, axis=0, keepdims=True)

    def scores(j, buf, heads=heads):
        s_ref, top_ref = buf
        kj = key_block(j)
        for hd in heads:
            s = _dot(kj, qh_ref[hd])
            s_ref[hd] = s
            top_ref[hd:hd + 1, :] = col_max([s[c] for c in chunks])

    def attend(j, buf, own, heads=heads):
        s_ref, top_ref = buf
        for hd in heads:
            def chunk(c):
                s = s_ref[hd, c, :]
                return causal(s, c.start) if own else s

            m_old = m_ref[hd:hd + 1, :]
            if own:
                m_new = jnp.maximum(m_old, col_max([chunk(c) for c in chunks]))
                shift = m_new
            else:
                pen = shift_ref[hd, pl.ds(j, 1), :] - bound_ref[hd:hd + 1, :]
                m_new = jnp.maximum(m_old, top_ref[hd:hd + 1, :] - pen)
                shift = m_new + pen
            m_ref[hd:hd + 1, :] = m_new
            for c in chunks:
                p_ref[0, hd, c, :] = jnp.exp2((chunk(c) - shift).astype(BF16))
            pv = _dot(vt_ref[0, j, vt_rows[hd], :], p_ref[0, hd])
            acc_ref[vt_rows[hd], :] = jnp.exp2(m_old - m_new) * acc_ref[vt_rows[hd], :] + pv

    @pl.when(jnp.logical_not(in_range))
    def _():
        m_ref[...] = jnp.full(m_ref.shape, MOBA_M_INIT, F32)
        acc_ref[...] = jnp.zeros_like(acc_ref)
        even = (s_even_ref, top_even_ref)
        odd = (s_odd_ref, top_odd_ref)
        scores(0, even)

        def body(t, carry):
            j = 2 * t
            scores(j + 1, odd)
            attend(j, even, own=False)
            scores(jnp.minimum(j + 2, qi), even)
            attend(j + 1, odd, own=False)
            return carry

        lax.fori_loop(0, (qi + 1) // 2, body, 0)
        attend(qi, even, own=True)
        finish()


def _moba(mqt, mk, mvt):
    b, s, _ = mk.shape
    nb = s // MOBA_BLOCK
    tq = MOBA_BLOCK
    vt_rows = MOBA_HEADS * MOBA_VT_ROWS
    block_stat_buf = pltpu.VMEM((nb, MOBA_WIDTH), F32)
    scores_buf = pltpu.VMEM((MOBA_HEADS, MOBA_BLOCK, tq), F32)
    head_rows_buf = pltpu.VMEM((8, tq), F32)
    return pl.pallas_call(
        _moba_kernel,
        grid=(b, nb),
        in_specs=[
            pl.BlockSpec((1, MOBA_WIDTH, tq), lambda bi, qi: (bi, 0, qi)),
            pl.BlockSpec((1, s, MOBA_WIDTH), lambda bi, qi: (bi, 0, 0)),
            pl.BlockSpec((1, nb, vt_rows, MOBA_BLOCK), lambda bi, qi: (bi, 0, 0, 0)),
        ],
        out_specs=pl.BlockSpec((1, tq, MOBA_WIDTH), lambda bi, qi: (bi, qi, 0)),
        out_shape=jax.ShapeDtypeStruct((b, s, MOBA_WIDTH), BF16),
        scratch_shapes=[
            block_stat_buf, block_stat_buf,
            pltpu.VMEM((MOBA_HEADS, MOBA_WIDTH, tq), BF16),
            pltpu.VMEM((MOBA_HEADS, nb, tq), F32),
            head_rows_buf,
            head_rows_buf,
            pltpu.VMEM((vt_rows, tq), F32),
            pltpu.VMEM((2, MOBA_HEADS, MOBA_BLOCK, tq), BF16),
            scores_buf, scores_buf,
            head_rows_buf, head_rows_buf,
        ],
        compiler_params=_params("parallel", "arbitrary"),
        name="moba",
    )(mqt, mk, mvt)


def _tail_kernel(x_ref, yr_ref, yp_ref, ym_ref, p_ref, wout_ref,
                 ffn_gain_ref, wg_ref, wu_ref, wd_ref,
                 ple_gain_ref, ple_wgate_ref, ple_wproj_ref, final_gain_ref,
                 o_ref, act_ref, *, final):
    x = x_ref[...]
    x = x + _dot(yr_ref[...], wout_ref[:RET_WIDTH, :])
    x = x + _dot(yp_ref[...], wout_ref[RET_WIDTH:RET_WIDTH + POOL_WIDTH, :])
    x = x + _dot(ym_ref[...], wout_ref[RET_WIDTH + POOL_WIDTH:, :])
    emb = _dot(p_ref[...].astype(BF16), ple_wproj_ref[...])
    x = _swiglu_half_step(x, ffn_gain_ref[...], wg_ref, wu_ref, wd_ref, act_ref)
    h = _rms(x, ple_gain_ref[...]).astype(BF16)
    x = x + jax.nn.sigmoid(_dot(h, ple_wgate_ref[...])) * emb
    if final:
        x = _rms(x, final_gain_ref[...])
    o_ref[...] = x


def _tail(x, y_ret, y_pool, y_moba, p, w_out, ffn, ple_gain, ple_w_gate, ple_w_proj,
          final_gain, layer, final):
    tokens = x.shape[0]
    tm = TOKEN_TILE
    ffn_gain, wg, wu, wd = ffn
    tok = lambda width: pl.BlockSpec((tm, width), lambda i: (i, 0))
    per_layer = lambda *shape: _resident((None,) + shape, lambda i: (layer,) + (0,) * len(shape))
    return pl.pallas_call(
        functools.partial(_tail_kernel, final=final),
        grid=(tokens // tm,),
        in_specs=[
            tok(D_MODEL), tok(RET_WIDTH), tok(POOL_WIDTH), tok(MOBA_WIDTH),
            pl.BlockSpec((None, tm, PLE_DIM), lambda i: (layer, i, 0)),
            per_layer(D_MODEL, D_MODEL),
            per_layer(1, D_MODEL), per_layer(D_MODEL, D_FF), per_layer(D_MODEL, D_FF),
            per_layer(D_FF, D_MODEL),
            per_layer(1, D_MODEL), per_layer(D_MODEL, D_MODEL), per_layer(PLE_DIM, D_MODEL),
            _resident((1, D_MODEL), lambda i: (0, 0)),
        ],
        out_specs=tok(D_MODEL),
        out_shape=jax.ShapeDtypeStruct(x.shape, F32),
        scratch_shapes=[pltpu.VMEM((tm, D_FF), BF16)],
        compiler_params=_params("parallel"),
        name="tail",
    )(x, y_ret, y_pool, y_moba, p, w_out, ffn_gain, wg, wu, wd,
      ple_gain, ple_w_gate, ple_w_proj, final_gain)


def _block_diag(w):
    depth, groups, d, _ = w.shape
    eye = jnp.eye(groups, dtype=w.dtype)
    return jnp.einsum("lgcd,gh->lgchd", w, eye).reshape(depth, groups * d, groups * d)


def kernel(x, p, norm_ffn1, ffn1_w_gate, ffn1_w_up, ffn1_w_down, norm_mix, w_in, pool_w, pool_scale, w_out, norm_ffn2, ffn2_w_gate, ffn2_w_up, ffn2_w_down, norm_ple, ple_w_gate, ple_w_proj, norm_final):
    b, s, d = x.shape
    depth = w_in.shape[0]
    tokens = b * s
    bf = lambda w: w.astype(BF16)
    row = lambda g: g.reshape(g.shape[0], 1, g.shape[1])

    ffn1 = (row(norm_ffn1), bf(ffn1_w_gate), bf(ffn1_w_up), bf(ffn1_w_down))
    ffn2 = (row(norm_ffn2), bf(ffn2_w_gate), bf(ffn2_w_up), bf(ffn2_w_down))
    w_in_b, w_out_b = bf(w_in), bf(w_out)
    pool_w_b = bf(_block_diag(pool_w))
    ple_gate_b, ple_proj_b = bf(ple_w_gate), bf(ple_w_proj)
    norm_mix_r, norm_ple_r, pool_scale_r = row(norm_mix), row(norm_ple), row(pool_scale)
    final_gain = norm_final.reshape(1, d)
    p_flat = p.reshape(depth, tokens, PLE_DIM)
    cos, sin = _rotation_tables(s)
    ret_tables = _retention_tables()

    xf = x.reshape(tokens, d)
    for i in range(depth):
        xf = _ffn(xf, *ffn1, i)
        rq, rk, rv, sg, y_pool, mqt, mk, mvt = _proj(
            xf.reshape(b, s, d), norm_mix_r, w_in_b, cos, sin, pool_w_b, pool_scale_r, i)
        y_ret = _retention(rq, rk, rv, sg, *ret_tables)
        y_moba = _moba(mqt, mk, mvt)
        xf = _tail(xf, y_ret.reshape(tokens, RET_WIDTH), y_pool.reshape(tokens, POOL_WIDTH),
                   y_moba.reshape(tokens, MOBA_WIDTH), p_flat, w_out_b, ffn2,
                   norm_ple_r, ple_gate_b, ple_proj_b, final_gain, i, i == depth - 1)
    return xf.reshape(b, s, d)
```
